```python
import jax, jax.numpy as jnp
from jax import lax
import numpy as np

D_MODEL = 2048
BATCH = 8
SEQ = 4096
DEPTH = 2

CHUNK = 64
N_PREV = 8
N_BAND = N_PREV + 1
MEM_LEN = 256
HEAD_DIM = 128
W_A = 512
CONV_WIDTH = 3
H_B = 8
W_B = H_B * HEAD_DIM
H_M = 4
W_M = H_M * HEAD_DIM
N_BRANCH = 3
MAX_REL = 256
EPS = 1e-6
SPLIT_SIZES = (W_A, W_A, W_A, W_A,
               W_B, W_B, W_B, W_B,
               W_M, W_M,
               D_MODEL, D_MODEL, D_MODEL)
N_IN = sum(SPLIT_SIZES)

kernel_name = "hybrid_gated_conv_chunkattn_memory"


def rms_norm(x, g):
    xf = x.astype(jnp.float32)
    y = xf * lax.rsqrt(jnp.mean(xf * xf, axis=-1, keepdims=True) + EPS)
    return (y * g.astype(jnp.float32)).astype(x.dtype)


def short_conv_mixer(b, c, xin, conv_w):
    u = c * xin
    s = u.shape[1]
    up = jnp.pad(u, ((0, 0), (CONV_WIDTH - 1, 0), (0, 0)))
    y = up[:, 0:s] * conv_w[0]
    for tap in range(1, CONV_WIDTH):
        y = y + up[:, tap:tap + s] * conv_w[tap]
    return b * y


def relative_bias(rel_table):
    j = jnp.arange(N_BAND, dtype=jnp.int32)[:, None, None]
    iq = jnp.arange(CHUNK, dtype=jnp.int32)[None, :, None]
    ik = jnp.arange(CHUNK, dtype=jnp.int32)[None, None, :]
    rel = (N_PREV - j) * CHUNK + iq - ik
    idx = jnp.clip(rel, -MAX_REL, MAX_REL) + MAX_REL
    bias = rel_table[idx]
    return jnp.transpose(bias, (3, 1, 0, 2))[:, None]


def chunk_band_attention(q, k, v, q_g, k_g, rel_table):
    bsz, s, h, dh = q.shape
    nc = s // CHUNK
    q = rms_norm(q, q_g) * (dh ** -0.5)
    k = rms_norm(k, k_g)
    qc = q.reshape(bsz, nc, CHUNK, h, dh)
    pad = ((0, 0), (N_PREV, 0), (0, 0), (0, 0), (0, 0))
    kp = jnp.pad(k.reshape(bsz, nc, CHUNK, h, dh), pad)
    vp = jnp.pad(v.reshape(bsz, nc, CHUNK, h, dh), pad)
    scores = jnp.stack([jnp.einsum('bnqhd,bnkhd->bhnqk', qc, kp[:, j:j + nc])
                        for j in range(N_BAND)], axis=4).astype(jnp.float32)
    scores = scores + relative_bias(rel_table).astype(jnp.float32)
    valid = (jnp.arange(nc)[:, None] - N_PREV + jnp.arange(N_BAND)[None, :]) >= 0
    scores = jnp.where(valid.reshape(1, 1, nc, 1, N_BAND, 1), scores, -1e30)
    p = jax.nn.softmax(scores.reshape(bsz, h, nc, CHUNK, N_BAND * CHUNK), axis=-1)
    p = p.reshape(bsz, h, nc, CHUNK, N_BAND, CHUNK).astype(v.dtype)
    out = jnp.einsum('bhnqk,bnkhd->bnqhd', p[:, :, :, :, 0], vp[:, 0:nc])
    for j in range(1, N_BAND):
        out = out + jnp.einsum('bhnqk,bnkhd->bnqhd', p[:, :, :, :, j], vp[:, j:j + nc])
    return out.reshape(bsz, s, h * dh)


def memory_attention(q, mk, mv, q_g, k_g):
    bsz, s, h, dh = q.shape
    q = rms_norm(q, q_g) * (dh ** -0.5)
    mk = rms_norm(mk, k_g)
    sc = jnp.einsum('bshd,bmhd->bhsm', q, mk).astype(jnp.float32)
    p = jax.nn.softmax(sc, axis=-1).astype(mv.dtype)
    return jnp.einsum('bhsm,bmhd->bshd', p, mv).reshape(bsz, s, h * dh)


def setup_inputs(seed: int = 0) -> dict:
    key = jax.random.key(seed)
    ks = jax.random.split(key, 20)
    f32 = jnp.float32
    nrm = lambda k, shape, scale: jax.random.normal(k, shape, f32) * scale
    return {
        "x": nrm(ks[0], (BATCH, SEQ, D_MODEL), 1.0),
        "mem": nrm(ks[1], (BATCH, MEM_LEN, D_MODEL), 1.0),
        "norm_g": 1.0 + nrm(ks[2], (DEPTH, D_MODEL), 0.02),
        "w_in": nrm(ks[3], (DEPTH, D_MODEL, N_IN), D_MODEL ** -0.5),
        "b_gate": nrm(ks[4], (DEPTH, N_BRANCH, D_MODEL), 0.02),
        "conv_w": nrm(ks[5], (DEPTH, CONV_WIDTH, W_A), CONV_WIDTH ** -0.5),
        "q_norm_g": 1.0 + nrm(ks[6], (DEPTH, HEAD_DIM), 0.02),
        "k_norm_g": 1.0 + nrm(ks[7], (DEPTH, HEAD_DIM), 0.02),
        "rel_table": nrm(ks[8], (DEPTH, 2 * MAX_REL + 1, H_B), 0.5),
        "mem_norm_g": 1.0 + nrm(ks[9], (DEPTH, D_MODEL), 0.02),
        "w_mem_kv": nrm(ks[10], (DEPTH, D_MODEL, 2 * W_M), D_MODEL ** -0.5),
        "mq_norm_g": 1.0 + nrm(ks[11], (DEPTH, HEAD_DIM), 0.02),
        "mk_norm_g": 1.0 + nrm(ks[12], (DEPTH, HEAD_DIM), 0.02),
        "w_branch_a": nrm(ks[13], (DEPTH, W_A, D_MODEL), W_A ** -0.5),
        "w_branch_b": nrm(ks[14], (DEPTH, W_B, D_MODEL), W_B ** -0.5),
        "w_branch_m": nrm(ks[15], (DEPTH, W_M, D_MODEL), W_M ** -0.5),
        "w_out": nrm(ks[16], (DEPTH, D_MODEL, D_MODEL), (2.0 * DEPTH * D_MODEL) ** -0.5),
    }


def reference(x, mem, norm_g, w_in, b_gate, conv_w, q_norm_g, k_norm_g, rel_table,
              mem_norm_g, w_mem_kv, mq_norm_g, mk_norm_g, w_branch_a, w_branch_b,
              w_branch_m, w_out):
    bsz, s, _ = x.shape
    offsets = np.cumsum(SPLIT_SIZES)[:-1].tolist()
    for l in range(DEPTH):
        h = rms_norm(x, norm_g[l])
        proj = h @ w_in[l]
        (a_b, a_c, a_x, a_z, q, k, v, b_z, mq, m_z,
         g_a, g_b, g_m) = jnp.split(proj, offsets, axis=-1)

        ya = short_conv_mixer(a_b, a_c, a_x, conv_w[l]) * jax.nn.silu(a_z)
        ya = ya @ w_branch_a[l]

        yb = chunk_band_attention(q.reshape(bsz, s, H_B, HEAD_DIM),
                                  k.reshape(bsz, s, H_B, HEAD_DIM),
                                  v.reshape(bsz, s, H_B, HEAD_DIM),
                                  q_norm_g[l], k_norm_g[l], rel_table[l])
        yb = (yb * jax.nn.silu(b_z)) @ w_branch_b[l]

        mkv = rms_norm(mem, mem_norm_g[l]) @ w_mem_kv[l]
        mk, mv = jnp.split(mkv, 2, axis=-1)
        ym = memory_attention(mq.reshape(bsz, s, H_M, HEAD_DIM),
                              mk.reshape(bsz, MEM_LEN, H_M, HEAD_DIM),
                              mv.reshape(bsz, MEM_LEN, H_M, HEAD_DIM),
                              mq_norm_g[l], mk_norm_g[l])
        ym = (ym * jax.nn.silu(m_z)) @ w_branch_m[l]

        merged = (jax.nn.sigmoid(g_a + b_gate[l, 0]) * ya
                  + jax.nn.sigmoid(g_b + b_gate[l, 1]) * yb
                  + jax.nn.sigmoid(g_m + b_gate[l, 2]) * ym)
        x = x + merged @ w_out[l]
    return x
```

```python
import functools

import jax
import jax.numpy as jnp
from jax import lax
from jax.experimental import pallas as pl
from jax.experimental.pallas import tpu as pltpu

F32 = jnp.float32
BF16 = jnp.bfloat16

CHUNK = 64
N_PREV = 8
N_BAND = N_PREV + 1
HEAD_DIM = 128
W_A = 512
W_B = 1024
W_M = 512
CONV_WIDTH = 3
MAX_REL = 256
EPS = 1e-6
MASK_VALUE = -1e30

GROUP_CHUNKS = 2
GROUP_ROWS = GROUP_CHUNKS * CHUNK
WINDOW = (N_BAND + GROUP_CHUNKS - 1) * CHUNK
HALO = N_PREV * CHUNK
BIAS_LANES = 1024

VMEM_LIMIT_BYTES = 58 * 1024 * 1024


def _params(n_axes):
    return pltpu.CompilerParams(
        dimension_semantics=("arbitrary",) * n_axes,
        vmem_limit_bytes=VMEM_LIMIT_BYTES,
    )


def _sigmoid(x):
    return 1.0 / (1.0 + jnp.exp(-x))


def _rms(x, g):
    ms = jnp.mean(x * x, axis=-1, keepdims=True)
    return (x * lax.rsqrt(ms + EPS)) * g


def _bias_kernel(table_ref, o_ref, *, n_heads):
    l = pl.program_id(0)
    h = pl.program_id(1)
    n_rel = 2 * MAX_REL + 1
    base = l * (n_rel * n_heads) + h
    lane = lax.broadcasted_iota(jnp.int32, (8, BIAS_LANES), 1)
    rel = N_PREV * CHUNK - (lane - CHUNK)
    idx = jnp.clip(rel, -MAX_REL, MAX_REL) + MAX_REL
    lo = MAX_REL - (CHUNK - 1)

    def fill(r, acc):
        return jnp.where(idx == r, table_ref[base + r * n_heads], acc)

    f_row = lax.fori_loop(lo, n_rel, fill, jnp.zeros((8, BIAS_LANES), F32))
    f_rows = jnp.broadcast_to(f_row[0:1, :], (CHUNK, BIAS_LANES))
    col = lax.broadcasted_iota(jnp.int32, (CHUNK, WINDOW), 1)
    r0 = pltpu.roll(f_rows, BIAS_LANES - CHUNK, 1, stride=1, stride_axis=0)[:, :WINDOW]
    o_ref[0, 0, 0:CHUNK, :] = jnp.where(col < N_BAND * CHUNK, r0, MASK_VALUE)
    r1 = pltpu.roll(f_rows, 0, 1, stride=1, stride_axis=0)[:, :WINDOW]
    o_ref[0, 0, CHUNK:GROUP_ROWS, :] = jnp.where(col >= CHUNK, r1, MASK_VALUE)


def _bias_call(rel_table):
    n_layers, n_rel, n_heads = rel_table.shape
    return pl.pallas_call(
        functools.partial(_bias_kernel, n_heads=n_heads),
        grid=(n_layers, n_heads),
        in_specs=[pl.BlockSpec(memory_space=pltpu.SMEM)],
        out_specs=pl.BlockSpec((1, 1, GROUP_ROWS, WINDOW), lambda l, h: (l, h, 0, 0)),
        out_shape=jax.ShapeDtypeStruct((n_layers, n_heads, GROUP_ROWS, WINDOW), F32),
        compiler_params=_params(2),
        name="rel_bias",
    )(rel_table.reshape(-1))


def _memkv_kernel(mem_ref, g_ref, w_ref, kg_ref, mk_ref, mv_ref):
    h = _rms(mem_ref[0], g_ref[0]).astype(BF16)
    kv = jnp.dot(h, w_ref[0], preferred_element_type=F32)
    for hd in range(W_M // HEAD_DIM):
        cs = slice(hd * HEAD_DIM, (hd + 1) * HEAD_DIM)
        mk_ref[0, 0, :, cs] = _rms(kv[:, cs], kg_ref[0]).astype(BF16)
    mv_ref[0, 0] = kv[:, W_M:].astype(BF16)


def _memkv_call(mem, mem_norm_g, w_mem_kv, mk_norm_g):
    bsz, mem_len, d = mem.shape
    n_layers = w_mem_kv.shape[0]
    out = jax.ShapeDtypeStruct((n_layers, bsz, mem_len, W_M), BF16)
    return pl.pallas_call(
        _memkv_kernel,
        grid=(n_layers, bsz),
        in_specs=[
            pl.BlockSpec((1, mem_len, d), lambda l, b: (b, 0, 0)),
            pl.BlockSpec((1, 1, d), lambda l, b: (l, 0, 0)),
            pl.BlockSpec((1, d, 2 * W_M), lambda l, b: (l, 0, 0)),
            pl.BlockSpec((1, 1, HEAD_DIM), lambda l, b: (l, 0, 0)),
        ],
        out_specs=[
            pl.BlockSpec((1, 1, mem_len, W_M), lambda l, b: (l, b, 0, 0)),
            pl.BlockSpec((1, 1, mem_len, W_M), lambda l, b: (l, b, 0, 0)),
        ],
        out_shape=[out, out],
        compiler_params=_params(2),
        name="mem_kv",
    )(mem, mem_norm_g.reshape(n_layers, 1, d), w_mem_kv, mk_norm_g.reshape(n_layers, 1, HEAD_DIM))


NORM_ROWS = 128


def _inproj_kernel(x_ref, g_ref, w_ref, o_ref, h_ref):
    @pl.when(pl.program_id(1) == 0)
    def _():
        def body(c, carry):
            r = pl.multiple_of(c * NORM_ROWS, NORM_ROWS)
            h_ref[pl.ds(r, NORM_ROWS), :] = _rms(x_ref[pl.ds(r, NORM_ROWS), :], g_ref[0]).astype(BF16)
            return carry

        lax.fori_loop(0, h_ref.shape[0] // NORM_ROWS, body, 0)

    o_ref[...] = jnp.dot(h_ref[...], w_ref[0], preferred_element_type=F32).astype(o_ref.dtype)


def _inproj_call(xf, norm_g, w_in, layer, *, tm, tn):
    m, d = xf.shape
    n = w_in.shape[-1]
    return pl.pallas_call(
        _inproj_kernel,
        grid=(m // tm, n // tn),
        in_specs=[
            pl.BlockSpec((tm, d), lambda i, j: (i, 0)),
            pl.BlockSpec((1, 1, d), lambda i, j: (layer, 0, 0)),
            pl.BlockSpec((1, d, tn), lambda i, j: (layer, 0, j)),
        ],
        out_specs=pl.BlockSpec((tm, tn), lambda i, j: (i, j)),
        out_shape=jax.ShapeDtypeStruct((m, n), BF16),
        scratch_shapes=[pltpu.VMEM((tm, d), BF16)],
        compiler_params=_params(2),
        name="in_proj",
    )(xf, norm_g, w_in)


CONV_ROWS = 128
MEM_ROWS = 128


def _mixer_kernel(ab_ref, ac_ref, ax_ref, az_ref, ach_ref, axh_ref,
                  q_ref, k_ref, v_ref, bz_ref, kp_ref, vp_ref,
                  mq_ref, mz_ref, mk_ref, mv_ref,
                  bias_ref, cw_ref, qg_ref, kg_ref, mqg_ref,
                  o_ref, u_scr, kn_scr, vw_scr, *, t):
    first = pl.program_id(1) == 0
    scale = HEAD_DIM ** -0.5

    uh = ach_ref[...].astype(F32) * axh_ref[...].astype(F32)
    u_scr[0:8, :] = jnp.where(first, 0.0, uh[8:16, :])
    for r in range(0, t, CONV_ROWS):
        rs = slice(r, r + CONV_ROWS)
        u_scr[8 + r:8 + r + CONV_ROWS, :] = ac_ref[rs, :].astype(F32) * ax_ref[rs, :].astype(F32)
    for r in range(0, t, CONV_ROWS):
        rs = slice(r, r + CONV_ROWS)
        y = u_scr[6 + r:6 + r + CONV_ROWS, :] * cw_ref[0:1, :]
        y = y + u_scr[7 + r:7 + r + CONV_ROWS, :] * cw_ref[1:2, :]
        y = y + u_scr[8 + r:8 + r + CONV_ROWS, :] * cw_ref[2:3, :]
        z = az_ref[rs, :].astype(F32)
        ya = (ab_ref[rs, :].astype(F32) * y) * (z * _sigmoid(z))
        o_ref[rs, 0:W_A] = ya.astype(o_ref.dtype)

    kmin = jnp.where(first, HALO, 0)
    col = lax.broadcasted_iota(jnp.int32, (GROUP_ROWS, WINDOW), 1)

    def head_b(h, carry):
        c0 = pl.multiple_of(h * HEAD_DIM, HEAD_DIM)
        hs = pl.ds(c0, HEAD_DIM)
        kn_scr[0:HALO, :] = _rms(kp_ref[:, hs].astype(F32), kg_ref[...]).astype(BF16)
        kn_scr[HALO:HALO + t, :] = _rms(k_ref[:, hs].astype(F32), kg_ref[...]).astype(BF16)
        vw_scr[0:HALO, :] = vp_ref[:, hs]
        vw_scr[HALO:HALO + t, :] = v_ref[:, hs]
        bias = bias_ref[h]
        for g in range(t // GROUP_ROWS):
            r0 = g * GROUP_ROWS
            rs = slice(r0, r0 + GROUP_ROWS)
            ws = slice(r0, r0 + WINDOW)
            qn = (_rms(q_ref[rs, hs].astype(F32), qg_ref[...]) * scale).astype(BF16)
            s = lax.dot_general(qn, kn_scr[ws, :], (((1,), (1,)), ((), ())),
                                preferred_element_type=F32)
            s = s + bias
            s = jnp.where(col + r0 >= kmin, s, MASK_VALUE)
            m = jnp.max(s, axis=-1, keepdims=True)
            p = jnp.exp(s - m)
            den = jnp.sum(p, axis=-1, keepdims=True)
            o = jnp.dot(p.astype(BF16), vw_scr[ws, :], preferred_element_type=F32) / den
            z = bz_ref[rs, hs].astype(F32)
            o_ref[rs, pl.ds(pl.multiple_of(W_A + c0, HEAD_DIM), HEAD_DIM)] = (
                o * (z * _sigmoid(z))).astype(o_ref.dtype)
        return carry

    lax.fori_loop(0, W_B // HEAD_DIM, head_b, 0)

    def head_m(h, carry):
        c0 = pl.multiple_of(h * HEAD_DIM, HEAD_DIM)
        hs = pl.ds(c0, HEAD_DIM)
        mk = mk_ref[0, 0, :, hs]
        mv = mv_ref[0, 0, :, hs]
        for r0 in range(0, t, MEM_ROWS):
            rs = slice(r0, r0 + MEM_ROWS)
            qn = (_rms(mq_ref[rs, hs].astype(F32), mqg_ref[...]) * scale).astype(BF16)
            s = lax.dot_general(qn, mk, (((1,), (1,)), ((), ())), preferred_element_type=F32)
            m = jnp.max(s, axis=-1, keepdims=True)
            p = jnp.exp(s - m)
            den = jnp.sum(p, axis=-1, keepdims=True)
            o = jnp.dot(p.astype(BF16), mv, preferred_element_type=F32) / den
            z = mz_ref[rs, hs].astype(F32)
            o_ref[rs, pl.ds(pl.multiple_of(W_A + W_B + c0, HEAD_DIM), HEAD_DIM)] = (
                o * (z * _sigmoid(z))).astype(o_ref.dtype)
        return carry

    lax.fori_loop(0, W_M // HEAD_DIM, head_m, 0)


def _mixer_call(proj, mk, mv, bias, conv_w, q_g, k_g, mq_g, layer, *, bsz, seq, t, cols):
    tiles = seq // t
    n_heads = W_B // HEAD_DIM
    mem_len = mk.shape[2]
    halo_blocks = t // 16

    def row(b, i):
        return b * tiles + i

    def prev_row(b, i):
        return b * tiles + jnp.maximum(i - 1, 0)

    def halo_row(b, i):
        return jnp.maximum((b * tiles + i) * halo_blocks - 1, 0)

    def tile_spec(width, col_block, row_fn=row, rows=t):
        return pl.BlockSpec((rows, width), lambda b, i: (row_fn(b, i), col_block))

    ca, cb, cm = cols["conv"] // W_A, cols["attn"] // W_B, cols["mem"] // W_M
    in_specs = [
        tile_spec(W_A, ca), tile_spec(W_A, ca + 1), tile_spec(W_A, ca + 2), tile_spec(W_A, ca + 3),
        tile_spec(W_A, ca + 1, halo_row, 16), tile_spec(W_A, ca + 2, halo_row, 16),
        tile_spec(W_B, cb), tile_spec(W_B, cb + 1), tile_spec(W_B, cb + 2), tile_spec(W_B, cb + 3),
        tile_spec(W_B, cb + 1, prev_row), tile_spec(W_B, cb + 2, prev_row),
        tile_spec(W_M, cm), tile_spec(W_M, cm + 1),
        pl.BlockSpec((1, 1, mem_len, W_M), lambda b, i: (layer, b, 0, 0)),
        pl.BlockSpec((1, 1, mem_len, W_M), lambda b, i: (layer, b, 0, 0)),
        pl.BlockSpec((None, n_heads, GROUP_ROWS, WINDOW), lambda b, i: (layer, 0, 0, 0)),
        pl.BlockSpec((None, CONV_WIDTH, W_A), lambda b, i: (layer, 0, 0)),
        pl.BlockSpec((None, 1, HEAD_DIM), lambda b, i: (layer, 0, 0)),
        pl.BlockSpec((None, 1, HEAD_DIM), lambda b, i: (layer, 0, 0)),
        pl.BlockSpec((None, 1, HEAD_DIM), lambda b, i: (layer, 0, 0)),
    ]
    width = W_A + W_B + W_M
    return pl.pallas_call(
        functools.partial(_mixer_kernel, t=t),
        grid=(bsz, tiles),
        in_specs=in_specs,
        out_specs=pl.BlockSpec((t, width), lambda b, i: (row(b, i), 0)),
        out_shape=jax.ShapeDtypeStruct((bsz * seq, width), BF16),
        scratch_shapes=[
            pltpu.VMEM((t + 8, W_A), F32),
            pltpu.VMEM((HALO + t, HEAD_DIM), BF16),
            pltpu.VMEM((HALO + t, HEAD_DIM), BF16),
        ],
        compiler_params=_params(2),
        name="mixers",
    )(proj, proj, proj, proj, proj, proj, proj, proj, proj, proj, proj, proj, proj, proj,
      mk, mv, bias, conv_w, q_g, k_g, mq_g)


MERGE_COLS = 512


def _merge_kernel(y_ref, ga_ref, gb_ref, gm_ref, x_ref, bg_ref, wa_ref, wb_ref, wm_ref, wo_ref,
                  o_ref, m_scr):
    d = o_ref.shape[1]
    for c in range(0, d, MERGE_COLS):
        cs = slice(c, c + MERGE_COLS)
        ya = jnp.dot(y_ref[:, 0:W_A], wa_ref[:, cs], preferred_element_type=F32)
        acc = _sigmoid(ga_ref[:, cs].astype(F32) + bg_ref[0:1, cs]) * ya
        yb = jnp.dot(y_ref[:, W_A:W_A + W_B], wb_ref[:, cs], preferred_element_type=F32)
        acc = acc + _sigmoid(gb_ref[:, cs].astype(F32) + bg_ref[1:2, cs]) * yb
        ym = jnp.dot(y_ref[:, W_A + W_B:], wm_ref[:, cs], preferred_element_type=F32)
        acc = acc + _sigmoid(gm_ref[:, cs].astype(F32) + bg_ref[2:3, cs]) * ym
        m_scr[:, cs] = acc.astype(BF16)
    o_ref[...] = x_ref[...] + jnp.dot(m_scr[...], wo_ref[...], preferred_element_type=F32)


def _merge_call(y, proj, xf, b_gate, w_a, w_b, w_m, w_o, layer, *, tm, gate_col):
    m, d = xf.shape
    gblk = gate_col // d
    resident = dict(pipeline_mode=pl.Buffered(1))
    return pl.pallas_call(
        _merge_kernel,
        grid=(m // tm,),
        in_specs=[
            pl.BlockSpec((tm, W_A + W_B + W_M), lambda i: (i, 0)),
            pl.BlockSpec((tm, d), lambda i: (i, gblk)),
            pl.BlockSpec((tm, d), lambda i: (i, gblk + 1)),
            pl.BlockSpec((tm, d), lambda i: (i, gblk + 2)),
            pl.BlockSpec((tm, d), lambda i: (i, 0)),
            pl.BlockSpec((None, 3, d), lambda i: (layer, 0, 0)),
            pl.BlockSpec((None, W_A, d), lambda i: (layer, 0, 0), **resident),
            pl.BlockSpec((None, W_B, d), lambda i: (layer, 0, 0), **resident),
            pl.BlockSpec((None, W_M, d), lambda i: (layer, 0, 0), **resident),
            pl.BlockSpec((None, d, d), lambda i: (layer, 0, 0), **resident),
        ],
        out_specs=pl.BlockSpec((tm, d), lambda i: (i, 0)),
        out_shape=jax.ShapeDtypeStruct((m, d), F32),
        scratch_shapes=[pltpu.VMEM((tm, d), BF16)],
        compiler_params=_params(1),
        name="merge_out",
    )(y, proj, proj, proj, xf, b_gate, w_a, w_b, w_m, w_o)


def kernel(x, mem, norm_g, w_in, b_gate, conv_w, q_norm_g, k_norm_g, rel_table, mem_norm_g,
           w_mem_kv, mq_norm_g, mk_norm_g, w_branch_a, w_branch_b, w_branch_m, w_out):
    bsz, seq, d = x.shape
    n_layers = w_in.shape[0]
    n_tok = bsz * seq

    n_mix = 4 * W_A + 4 * W_B + 2 * W_M
    cols = {"gate": 0, "conv": 3 * d, "attn": 3 * d + 4 * W_A, "mem": 3 * d + 4 * W_A + 4 * W_B}
    w_in_p = jnp.concatenate([w_in[:, :, n_mix:], w_in[:, :, :n_mix]], axis=-1).astype(BF16)
    w_kv = w_mem_kv.astype(BF16)
    w_a, w_b, w_m, w_o = (w.astype(BF16) for w in (w_branch_a, w_branch_b, w_branch_m, w_out))
    norm_g3 = norm_g.reshape(n_layers, 1, d)
    q_g = q_norm_g.reshape(n_layers, 1, HEAD_DIM)
    k_g = k_norm_g.reshape(n_layers, 1, HEAD_DIM)
    mq_g = mq_norm_g.reshape(n_layers, 1, HEAD_DIM)

    tm_in = min(1024, n_tok)
    tn_in = 1024
    t_mix = min(512, seq)
    tm_out = min(512, n_tok)

    bias = _bias_call(rel_table)
    mk, mv = _memkv_call(mem, mem_norm_g, w_kv, mk_norm_g)

    xf = x.reshape(n_tok, d)
    for layer in range(n_layers):
        proj = _inproj_call(xf, norm_g3, w_in_p, layer, tm=tm_in, tn=tn_in)
        y = _mixer_call(proj, mk, mv, bias, conv_w, q_g, k_g, mq_g, layer,
                        bsz=bsz, seq=seq, t=t_mix, cols=cols)
        xf = _merge_call(y, proj, xf, b_gate, w_a, w_b, w_m, w_o, layer,
                         tm=tm_out, gate_col=cols["gate"])
    return xf.reshape(bsz, seq, d)
```

```python
import functools

import jax
import jax.numpy as jnp
from jax import lax
from jax.experimental import pallas as pl
from jax.experimental.pallas import tpu as pltpu

F32 = jnp.float32
BF16 = jnp.bfloat16

CHUNK = 64
N_PREV = 8
N_BAND = N_PREV + 1
HEAD_DIM = 128
W_A = 512
W_B = 1024
W_M = 512
CONV_WIDTH = 3
MAX_REL = 256
EPS = 1e-6
LOG2E = 1.4426950408889634
MASK_VALUE = -1e30

GROUP_CHUNKS = 2
GROUP_ROWS = GROUP_CHUNKS * CHUNK
WINDOW = (N_BAND + GROUP_CHUNKS - 1) * CHUNK
HALO = N_PREV * CHUNK
BIAS_LANES = 1024

VMEM_LIMIT_BYTES = 58 * 1024 * 1024


def _params(n_axes):
    return pltpu.CompilerParams(
        dimension_semantics=("arbitrary",) * n_axes,
        vmem_limit_bytes=VMEM_LIMIT_BYTES,
    )


def _sigmoid(x):
    return 1.0 / (1.0 + jnp.exp(-x))


def _silu(x):
    return x * _sigmoid(x)


def _rms(x, g):
    ms = jnp.mean(x * x, axis=-1, keepdims=True)
    return (x * lax.rsqrt(ms + EPS)) * g


def _head_rms(x, g, ones):
    ms = jnp.dot((x * x).astype(BF16), ones, preferred_element_type=F32) * (1.0 / HEAD_DIM)
    return (x * lax.rsqrt(ms + EPS)) * g


def _bias_kernel(table_ref, o_ref, *, n_heads):
    l = pl.program_id(0)
    h = pl.program_id(1)
    n_rel = 2 * MAX_REL + 1
    base = l * (n_rel * n_heads) + h
    lane = lax.broadcasted_iota(jnp.int32, (8, BIAS_LANES), 1)
    rel = N_PREV * CHUNK - (lane - CHUNK)
    idx = jnp.clip(rel, -MAX_REL, MAX_REL) + MAX_REL
    lo = MAX_REL - (CHUNK - 1)

    def fill(r, acc):
        return jnp.where(idx == r, table_ref[base + r * n_heads], acc)

    f_row = lax.fori_loop(lo, n_rel, fill, jnp.zeros((8, BIAS_LANES), F32))
    f_rows = jnp.broadcast_to(f_row[0:1, :] * LOG2E, (CHUNK, BIAS_LANES))
    col = lax.broadcasted_iota(jnp.int32, (CHUNK, WINDOW), 1)
    r0 = pltpu.roll(f_rows, BIAS_LANES - CHUNK, 1, stride=1, stride_axis=0)[:, :WINDOW]
    o_ref[0, 0, 0:CHUNK, :] = jnp.where(col < N_BAND * CHUNK, r0, MASK_VALUE)
    r1 = pltpu.roll(f_rows, 0, 1, stride=1, stride_axis=0)[:, :WINDOW]
    o_ref[0, 0, CHUNK:GROUP_ROWS, :] = jnp.where(col >= CHUNK, r1, MASK_VALUE)


def _bias_call(rel_table):
    n_layers, n_rel, n_heads = rel_table.shape
    return pl.pallas_call(
        functools.partial(_bias_kernel, n_heads=n_heads),
        grid=(n_layers, n_heads),
        in_specs=[pl.BlockSpec(memory_space=pltpu.SMEM)],
        out_specs=pl.BlockSpec((1, 1, GROUP_ROWS, WINDOW), lambda l, h: (l, h, 0, 0)),
        out_shape=jax.ShapeDtypeStruct((n_layers, n_heads, GROUP_ROWS, WINDOW), F32),
        compiler_params=_params(2),
        name="rel_bias",
    )(rel_table.reshape(-1))


def _memkv_kernel(mem_ref, g_ref, w_ref, kg_ref, mk_ref, mv_ref):
    h = _rms(mem_ref[0], g_ref[0]).astype(BF16)
    kv = jnp.dot(h, w_ref[0], preferred_element_type=F32)
    for hd in range(W_M // HEAD_DIM):
        cs = slice(hd * HEAD_DIM, (hd + 1) * HEAD_DIM)
        mk_ref[0, 0, :, cs] = _rms(kv[:, cs], kg_ref[0]).astype(BF16)
        mv_ref[0, 0, hd, :, 0:HEAD_DIM] = kv[:, W_M + hd * HEAD_DIM:W_M + (hd + 1) * HEAD_DIM].astype(BF16)
        mv_ref[0, 0, hd, :, HEAD_DIM:] = jnp.ones((kv.shape[0], HEAD_DIM), BF16)


def _memkv_call(mem, mem_norm_g, w_mem_kv, mk_norm_g):
    bsz, mem_len, d = mem.shape
    n_layers = w_mem_kv.shape[0]
    n_heads = W_M // HEAD_DIM
    return pl.pallas_call(
        _memkv_kernel,
        grid=(n_layers, bsz),
        in_specs=[
            pl.BlockSpec((1, mem_len, d), lambda l, b: (b, 0, 0)),
            pl.BlockSpec((1, 1, d), lambda l, b: (l, 0, 0)),
            pl.BlockSpec((1, d, 2 * W_M), lambda l, b: (l, 0, 0)),
            pl.BlockSpec((1, 1, HEAD_DIM), lambda l, b: (l, 0, 0)),
        ],
        out_specs=[
            pl.BlockSpec((1, 1, mem_len, W_M), lambda l, b: (l, b, 0, 0)),
            pl.BlockSpec((1, 1, n_heads, mem_len, 2 * HEAD_DIM), lambda l, b: (l, b, 0, 0, 0)),
        ],
        out_shape=[
            jax.ShapeDtypeStruct((n_layers, bsz, mem_len, W_M), BF16),
            jax.ShapeDtypeStruct((n_layers, bsz, n_heads, mem_len, 2 * HEAD_DIM), BF16),
        ],
        compiler_params=_params(2),
        name="mem_kv",
    )(mem, mem_norm_g.reshape(n_layers, 1, d), w_mem_kv, mk_norm_g.reshape(n_layers, 1, HEAD_DIM))


NORM_ROWS = 128


def _inproj_kernel(x_ref, g_ref, w_ref, o_ref, h_ref):
    @pl.when(pl.program_id(1) == 0)
    def _():
        def body(c, carry):
            r = pl.multiple_of(c * NORM_ROWS, NORM_ROWS)
            h_ref[pl.ds(r, NORM_ROWS), :] = _rms(x_ref[pl.ds(r, NORM_ROWS), :], g_ref[0]).astype(BF16)
            return carry

        lax.fori_loop(0, h_ref.shape[0] // NORM_ROWS, body, 0)

    o_ref[...] = jnp.dot(h_ref[...], w_ref[0], preferred_element_type=F32).astype(o_ref.dtype)


def _inproj_call(xf, norm_g, w_in, layer, *, tm, tn):
    m, d = xf.shape
    n = w_in.shape[-1]
    return pl.pallas_call(
        _inproj_kernel,
        grid=(m // tm, n // tn),
        in_specs=[
            pl.BlockSpec((tm, d), lambda i, j: (i, 0)),
            pl.BlockSpec((1, 1, d), lambda i, j: (layer, 0, 0)),
            pl.BlockSpec((1, d, tn), lambda i, j: (layer, 0, j)),
        ],
        out_specs=pl.BlockSpec((tm, tn), lambda i, j: (i, j)),
        out_shape=jax.ShapeDtypeStruct((m, n), BF16),
        scratch_shapes=[pltpu.VMEM((tm, d), BF16)],
        compiler_params=_params(2),
        name="in_proj",
    )(xf, norm_g, w_in)


STEP = 256
STEP_GROUPS = STEP // GROUP_ROWS
CARRY_ROWS = 8
MEM_ROWS = 128
MERGE_COLS = 512
GATE_BLOCK = 1024
KEY_LANES = 2 * HEAD_DIM
OUT_COLS = 256
NORM_AHEAD = 2

def _mix_merge_kernel(ab_ref, ac_ref, ax_ref, az_ref, q_ref, k_ref, v_ref, bz_ref, mq_ref, mz_ref,
                      mk_ref, mv_ref, bias_ref, cw_ref, qg_ref, kg_ref, mqg_ref,
                      ga0, ga1, gb0, gb1, gm0, gm1, x_ref, bg_ref, wa_ref, wb_ref, wm_ref, wo_ref,
                      o_ref, u_scr, kw_scr, vw_scr, qn_scr, mqn_scr, y_scr, yp_scr, m_scr, *, n_tiles, tiles_per_seq):
    t = STEP
    n_heads_b = W_B // HEAD_DIM
    n_heads_m = W_M // HEAD_DIM
    step = pl.program_id(0)
    tile = jnp.minimum(step, n_tiles - 1)
    seq_start = tile % tiles_per_seq == 0
    scale = HEAD_DIM ** -0.5 * LOG2E
    ones = jnp.ones((HEAD_DIM, HEAD_DIM), BF16)

    @pl.when(step == 0)
    def _():
        one_hot = jnp.where(lax.broadcasted_iota(jnp.int32, (t, HEAD_DIM), 1) == 0, 1.0, 0.0).astype(BF16)
        y_scr[...] = jnp.zeros(y_scr.shape, BF16)
        u_scr[...] = jnp.zeros(u_scr.shape, F32)
        for h in range(n_heads_b):
            kw_scr[h] = jnp.zeros(kw_scr.shape[1:], BF16)
            vw_scr[h] = jnp.zeros(vw_scr.shape[1:], BF16)
            qn_scr[h, :, HEAD_DIM:] = one_hot

    yp_scr[...] = y_scr[...]

    @pl.when(seq_start)
    def _():
        lane = lax.broadcasted_iota(jnp.int32, (HALO, KEY_LANES), 1)
        pen = jnp.where(lane == HEAD_DIM, MASK_VALUE, 0.0).astype(BF16)
        u_scr[t:t + CARRY_ROWS, :] = jnp.zeros((CARRY_ROWS, W_A), F32)
        for h in range(n_heads_b):
            kw_scr[h, t:, :] = pen
            vw_scr[h, t:, :] = jnp.zeros((HALO, KEY_LANES), BF16)

    d = o_ref.shape[1]
    gates = ((ga0, ga1), (gb0, gb1), (gm0, gm1))

    def conv():
        u_scr[0:CARRY_ROWS, :] = u_scr[t:t + CARRY_ROWS, :]
        u_scr[CARRY_ROWS:CARRY_ROWS + t, :] = ac_ref[...].astype(F32) * ax_ref[...].astype(F32)
        y = u_scr[CARRY_ROWS - 2:CARRY_ROWS - 2 + t, :] * cw_ref[0:1, :]
        y = y + u_scr[CARRY_ROWS - 1:CARRY_ROWS - 1 + t, :] * cw_ref[1:2, :]
        y = y + u_scr[CARRY_ROWS:CARRY_ROWS + t, :] * cw_ref[2:3, :]
        ya = (ab_ref[...].astype(F32) * y) * _silu(az_ref[...].astype(F32))
        y_scr[:, 0:W_A] = ya.astype(BF16)

    def band_norm(h):
        hs = slice(h * HEAD_DIM, (h + 1) * HEAD_DIM)
        for r in range(0, HALO, t):
            kw_scr[h, r:r + t, :] = kw_scr[h, r + t:r + 2 * t, :]
            vw_scr[h, r:r + t, :] = vw_scr[h, r + t:r + 2 * t, :]
        kw_scr[h, HALO:, 0:HEAD_DIM] = _head_rms(k_ref[:, hs].astype(F32), kg_ref[...], ones).astype(BF16)
        kw_scr[h, HALO:, HEAD_DIM:] = jnp.zeros((t, HEAD_DIM), BF16)
        vw_scr[h, HALO:, 0:HEAD_DIM] = v_ref[:, hs]
        vw_scr[h, HALO:, HEAD_DIM:] = jnp.ones((t, HEAD_DIM), BF16)
        qn_scr[h, :, 0:HEAD_DIM] = _head_rms(q_ref[:, hs].astype(F32), qg_ref[...] * scale, ones).astype(BF16)

    def band_scores(h):
        ps = []
        for g in range(STEP_GROUPS):
            r0 = g * GROUP_ROWS
            s = lax.dot_general(qn_scr[h, r0:r0 + GROUP_ROWS, :], kw_scr[h, r0:r0 + WINDOW, :],
                                (((1,), (1,)), ((), ())), preferred_element_type=F32)
            s = s + bias_ref[h]
            ps.append(jnp.exp2(s - jnp.max(s, axis=-1, keepdims=True)).astype(BF16))
        return ps

    def band_values(h, ps):
        hs = slice(h * HEAD_DIM, (h + 1) * HEAD_DIM)
        for g, p in enumerate(ps):
            r0 = g * GROUP_ROWS
            rs = slice(r0, r0 + GROUP_ROWS)
            od = jnp.dot(p, vw_scr[h, r0:r0 + WINDOW, :], preferred_element_type=F32)
            o = od[:, 0:HEAD_DIM] / od[:, HEAD_DIM:]
            y_scr[rs, W_A + h * HEAD_DIM:W_A + (h + 1) * HEAD_DIM] = (
                o * _silu(bz_ref[rs, hs].astype(F32))).astype(BF16)

    def mem_norm(h):
        hs = slice(h * HEAD_DIM, (h + 1) * HEAD_DIM)
        mqn_scr[h] = _head_rms(mq_ref[:, hs].astype(F32), mqg_ref[...] * scale, ones).astype(BF16)

    def mem_scores(h):
        mk = mk_ref[0, 0, :, h * HEAD_DIM:(h + 1) * HEAD_DIM]
        ps = []
        for r0 in range(0, t, MEM_ROWS):
            s = lax.dot_general(mqn_scr[h, r0:r0 + MEM_ROWS, :], mk, (((1,), (1,)), ((), ())),
                                preferred_element_type=F32)
            ps.append(jnp.exp2(s - jnp.max(s, axis=-1, keepdims=True)).astype(BF16))
        return ps

    def mem_values(h, ps):
        hs = slice(h * HEAD_DIM, (h + 1) * HEAD_DIM)
        mv = mv_ref[0, 0, h]
        for i, p in enumerate(ps):
            rs = slice(i * MEM_ROWS, (i + 1) * MEM_ROWS)
            od = jnp.dot(p, mv, preferred_element_type=F32)
            o = od[:, 0:HEAD_DIM] / od[:, HEAD_DIM:]
            c = W_A + W_B + h * HEAD_DIM
            y_scr[rs, c:c + HEAD_DIM] = (o * _silu(mz_ref[rs, hs].astype(F32))).astype(BF16)

    def merge(j):
        c = j * MERGE_COLS
        cs = slice(c, c + MERGE_COLS)
        gs = slice(c % GATE_BLOCK, c % GATE_BLOCK + MERGE_COLS)
        blk = c // GATE_BLOCK
        pa = jnp.dot(yp_scr[:, 0:W_A], wa_ref[:, cs], preferred_element_type=F32)
        acc = _sigmoid(gates[0][blk][:, gs].astype(F32) + bg_ref[0:1, cs]) * pa
        pb = jnp.dot(yp_scr[:, W_A:W_A + W_B], wb_ref[:, cs], preferred_element_type=F32)
        acc = acc + _sigmoid(gates[1][blk][:, gs].astype(F32) + bg_ref[1:2, cs]) * pb
        pm = jnp.dot(yp_scr[:, W_A + W_B:], wm_ref[:, cs], preferred_element_type=F32)
        acc = acc + _sigmoid(gates[2][blk][:, gs].astype(F32) + bg_ref[2:3, cs]) * pm
        m_scr[:, cs] = acc.astype(BF16)

    def out(j):
        cs = slice(j * OUT_COLS, (j + 1) * OUT_COLS)
        o_ref[:, cs] = x_ref[:, cs] + jnp.dot(m_scr[...], wo_ref[:, cs], preferred_element_type=F32)

    heads = ([(band_norm, band_scores, band_values, h) for h in range(n_heads_b)]
             + [(mem_norm, mem_scores, mem_values, h) for h in range(n_heads_m)])
    fillers = ([functools.partial(merge, j) for j in range(d // MERGE_COLS)]
               + [functools.partial(out, j) for j in range(d // OUT_COLS)])
    assert len(fillers) == len(heads)
    conv()
    for norm, _, _, h in heads[:NORM_AHEAD]:
        norm(h)
    ps_next = heads[0][1](heads[0][3])
    for i, (_, _, values, h) in enumerate(heads):
        ps = ps_next
        if i + NORM_AHEAD < len(heads):
            norm_ahead, _, _, h_ahead = heads[i + NORM_AHEAD]
            norm_ahead(h_ahead)
        fillers[i]()
        if i + 1 < len(heads):
            _, scores_next, _, h_next = heads[i + 1]
            ps_next = scores_next(h_next)
        values(h, ps)


def _mix_merge_call(proj, xf, mk, mv, bias, conv_w, q_g, k_g, mq_g, b_gate, w_a, w_b, w_m, w_o,
                    layer, *, seq, cols):
    n_tok, d = xf.shape
    t = STEP
    n_tiles = n_tok // t
    tiles_per_seq = seq // t
    n_heads = W_B // HEAD_DIM
    n_heads_m = W_M // HEAD_DIM
    mem_len = mk.shape[2]

    def mix_tile(s):
        return jnp.minimum(s, n_tiles - 1)

    def merge_tile(s):
        return jnp.maximum(s - 1, 0)

    def mix_spec(width, col_block):
        return pl.BlockSpec((t, width), lambda s: (mix_tile(s), col_block))

    ca, cb, cm = cols["conv"] // W_A, cols["attn"] // W_B, cols["mem"] // W_M
    gblk = cols["gate"] // GATE_BLOCK
    resident = dict(pipeline_mode=pl.Buffered(1))
    gate_specs = [pl.BlockSpec((t, GATE_BLOCK), lambda s, j=j: (merge_tile(s), gblk + j))
                  for j in range(3 * d // GATE_BLOCK)]
    in_specs = [
        mix_spec(W_A, ca), mix_spec(W_A, ca + 1), mix_spec(W_A, ca + 2), mix_spec(W_A, ca + 3),
        mix_spec(W_B, cb), mix_spec(W_B, cb + 1), mix_spec(W_B, cb + 2), mix_spec(W_B, cb + 3),
        mix_spec(W_M, cm), mix_spec(W_M, cm + 1),
        pl.BlockSpec((1, 1, mem_len, W_M), lambda s: (layer, mix_tile(s) // tiles_per_seq, 0, 0)),
        pl.BlockSpec((1, 1, n_heads_m, mem_len, 2 * HEAD_DIM),
                     lambda s: (layer, mix_tile(s) // tiles_per_seq, 0, 0, 0)),
        pl.BlockSpec((None, n_heads, GROUP_ROWS, WINDOW), lambda s: (layer, 0, 0, 0), **resident),
        pl.BlockSpec((None, CONV_WIDTH, W_A), lambda s: (layer, 0, 0)),
        pl.BlockSpec((None, 1, HEAD_DIM), lambda s: (layer, 0, 0)),
        pl.BlockSpec((None, 1, HEAD_DIM), lambda s: (layer, 0, 0)),
        pl.BlockSpec((None, 1, HEAD_DIM), lambda s: (layer, 0, 0)),
    ] + gate_specs + [
        pl.BlockSpec((t, d), lambda s: (merge_tile(s), 0)),
        pl.BlockSpec((None, 3, d), lambda s: (layer, 0, 0)),
        pl.BlockSpec((None, W_A, d), lambda s: (layer, 0, 0), **resident),
        pl.BlockSpec((None, W_B, d), lambda s: (layer, 0, 0), **resident),
        pl.BlockSpec((None, W_M, d), lambda s: (layer, 0, 0), **resident),
        pl.BlockSpec((None, d, d), lambda s: (layer, 0, 0), **resident),
    ]
    return pl.pallas_call(
        functools.partial(_mix_merge_kernel, n_tiles=n_tiles, tiles_per_seq=tiles_per_seq),
        grid=(n_tiles + 1,),
        in_specs=in_specs,
        out_specs=pl.BlockSpec((t, d), lambda s: (merge_tile(s), 0)),
        out_shape=jax.ShapeDtypeStruct((n_tok, d), F32),
        scratch_shapes=[
            pltpu.VMEM((t + CARRY_ROWS, W_A), F32),
            pltpu.VMEM((n_heads, HALO + t, KEY_LANES), BF16),
            pltpu.VMEM((n_heads, HALO + t, 2 * HEAD_DIM), BF16),
            pltpu.VMEM((n_heads, t, KEY_LANES), BF16),
            pltpu.VMEM((n_heads_m, t, HEAD_DIM), BF16),
            pltpu.VMEM((t, W_A + W_B + W_M), BF16),
            pltpu.VMEM((t, W_A + W_B + W_M), BF16),
            pltpu.VMEM((t, d), BF16),
        ],
        compiler_params=_params(1),
        name="mix_merge",
    )(*([proj] * 10), mk, mv, bias, conv_w, q_g, k_g, mq_g, *([proj] * len(gate_specs)),
      xf, b_gate, w_a, w_b, w_m, w_o)


def kernel(x, mem, norm_g, w_in, b_gate, conv_w, q_norm_g, k_norm_g, rel_table, mem_norm_g,
           w_mem_kv, mq_norm_g, mk_norm_g, w_branch_a, w_branch_b, w_branch_m, w_out):
    bsz, seq, d = x.shape
    n_layers = w_in.shape[0]
    n_tok = bsz * seq
    assert seq % HALO == 0 and HALO % STEP == 0 and d == 2 * GATE_BLOCK

    cols = {"conv": 0, "attn": 4 * W_A, "mem": 4 * W_A + 4 * W_B, "gate": 4 * W_A + 4 * W_B + 2 * W_M}
    w_in_b = w_in.astype(BF16)
    w_kv = w_mem_kv.astype(BF16)
    w_a, w_b, w_m, w_o = (w.astype(BF16) for w in (w_branch_a, w_branch_b, w_branch_m, w_out))
    norm_g3 = norm_g.reshape(n_layers, 1, d)
    q_g = q_norm_g.reshape(n_layers, 1, HEAD_DIM)
    k_g = k_norm_g.reshape(n_layers, 1, HEAD_DIM)
    mq_g = mq_norm_g.reshape(n_layers, 1, HEAD_DIM)

    tm_in = min(1024, n_tok)
    tn_in = 1024

    bias = _bias_call(rel_table)
    mk, mv = _memkv_call(mem, mem_norm_g, w_kv, mk_norm_g)

    xf = x.reshape(n_tok, d)
    for layer in range(n_layers):
        proj = _inproj_call(xf, norm_g3, w_in_b, layer, tm=tm_in, tn=tn_in)
        xf = _mix_merge_call(proj, xf, mk, mv, bias, conv_w, q_g, k_g, mq_g, b_gate,
                             w_a, w_b, w_m, w_o, layer, seq=seq, cols=cols)
    return xf.reshape(bsz, seq, d)
```

```python
import functools

import jax
import jax.numpy as jnp
from jax import lax
from jax.experimental import pallas as pl
from jax.experimental.pallas import tpu as pltpu

F32 = jnp.float32
BF16 = jnp.bfloat16

CHUNK = 64
N_PREV = 8
N_BAND = N_PREV + 1
HEAD_DIM = 128
W_A = 512
W_B = 1024
W_M = 512
CONV_WIDTH = 3
MAX_REL = 256
EPS = 1e-6
LOG2E = 1.4426950408889634
MASK_VALUE = -1e30

GROUP_CHUNKS = 2
GROUP_ROWS = GROUP_CHUNKS * CHUNK
WINDOW = (N_BAND + GROUP_CHUNKS - 1) * CHUNK
HALO = N_PREV * CHUNK
BIAS_LANES = 1024

VMEM_LIMIT_BYTES = 58 * 1024 * 1024


def _params(n_axes):
    return pltpu.CompilerParams(
        dimension_semantics=("arbitrary",) * n_axes,
        vmem_limit_bytes=VMEM_LIMIT_BYTES,
    )


def _sigmoid(x):
    return 1.0 / (1.0 + jnp.exp2(x * -LOG2E))


def _silu(x):
    return x * _sigmoid(x)


def _rms(x, g):
    ms = jnp.mean(x * x, axis=-1, keepdims=True)
    return (x * lax.rsqrt(ms + EPS)) * g


def _head_rms(x, g, ones):
    ms = jnp.dot((x * x).astype(BF16), ones, preferred_element_type=F32) * (1.0 / HEAD_DIM)
    return (x * lax.rsqrt(ms + EPS)) * g


def _bias_kernel(table_ref, o_ref, *, n_heads):
    l = pl.program_id(0)
    h = pl.program_id(1)
    n_rel = 2 * MAX_REL + 1
    base = l * (n_rel * n_heads) + h
    lane = lax.broadcasted_iota(jnp.int32, (8, BIAS_LANES), 1)
    rel = N_PREV * CHUNK - (lane - CHUNK)
    idx = jnp.clip(rel, -MAX_REL, MAX_REL) + MAX_REL
    lo = MAX_REL - (CHUNK - 1)

    def fill(r, acc):
        return jnp.where(idx == r, table_ref[base + r * n_heads], acc)

    f_row = lax.fori_loop(lo, n_rel, fill, jnp.zeros((8, BIAS_LANES), F32))
    f_rows = jnp.broadcast_to(f_row[0:1, :] * LOG2E, (CHUNK, BIAS_LANES))
    col = lax.broadcasted_iota(jnp.int32, (CHUNK, WINDOW), 1)
    r0 = pltpu.roll(f_rows, BIAS_LANES - CHUNK, 1, stride=1, stride_axis=0)[:, :WINDOW]
    o_ref[0, 0, 0:CHUNK, :] = jnp.where(col < N_BAND * CHUNK, r0, MASK_VALUE)
    r1 = pltpu.roll(f_rows, 0, 1, stride=1, stride_axis=0)[:, :WINDOW]
    o_ref[0, 0, CHUNK:GROUP_ROWS, :] = jnp.where(col >= CHUNK, r1, MASK_VALUE)


def _bias_call(rel_table):
    n_layers, n_rel, n_heads = rel_table.shape
    return pl.pallas_call(
        functools.partial(_bias_kernel, n_heads=n_heads),
        grid=(n_layers, n_heads),
        in_specs=[pl.BlockSpec(memory_space=pltpu.SMEM)],
        out_specs=pl.BlockSpec((1, 1, GROUP_ROWS, WINDOW), lambda l, h: (l, h, 0, 0)),
        out_shape=jax.ShapeDtypeStruct((n_layers, n_heads, GROUP_ROWS, WINDOW), F32),
        compiler_params=_params(2),
        name="rel_bias",
    )(rel_table.reshape(-1))


def _memkv_kernel(mem_ref, g_ref, w_ref, kg_ref, mk_ref, mv_ref):
    h = _rms(mem_ref[0], g_ref[0]).astype(BF16)
    kv = jnp.dot(h, w_ref[0], preferred_element_type=F32)
    for hd in range(W_M // HEAD_DIM):
        cs = slice(hd * HEAD_DIM, (hd + 1) * HEAD_DIM)
        mk_ref[0, 0, :, cs] = _rms(kv[:, cs], kg_ref[0]).astype(BF16)
        mv_ref[0, 0, hd, :, 0:HEAD_DIM] = kv[:, W_M + hd * HEAD_DIM:W_M + (hd + 1) * HEAD_DIM].astype(BF16)
        mv_ref[0, 0, hd, :, HEAD_DIM:] = jnp.ones((kv.shape[0], HEAD_DIM), BF16)


def _memkv_call(mem, mem_norm_g, w_mem_kv, mk_norm_g):
    bsz, mem_len, d = mem.shape
    n_layers = w_mem_kv.shape[0]
    n_heads = W_M // HEAD_DIM
    return pl.pallas_call(
        _memkv_kernel,
        grid=(n_layers, bsz),
        in_specs=[
            pl.BlockSpec((1, mem_len, d), lambda l, b: (b, 0, 0)),
            pl.BlockSpec((1, 1, d), lambda l, b: (l, 0, 0)),
            pl.BlockSpec((1, d, 2 * W_M), lambda l, b: (l, 0, 0)),
            pl.BlockSpec((1, 1, HEAD_DIM), lambda l, b: (l, 0, 0)),
        ],
        out_specs=[
            pl.BlockSpec((1, 1, mem_len, W_M), lambda l, b: (l, b, 0, 0)),
            pl.BlockSpec((1, 1, n_heads, mem_len, 2 * HEAD_DIM), lambda l, b: (l, b, 0, 0, 0)),
        ],
        out_shape=[
            jax.ShapeDtypeStruct((n_layers, bsz, mem_len, W_M), BF16),
            jax.ShapeDtypeStruct((n_layers, bsz, n_heads, mem_len, 2 * HEAD_DIM), BF16),
        ],
        compiler_params=_params(2),
        name="mem_kv",
    )(mem, mem_norm_g.reshape(n_layers, 1, d), w_mem_kv, mk_norm_g.reshape(n_layers, 1, HEAD_DIM))


NORM_ROWS = 128
NORM_CHUNK = 208
NORM_SUB = 16
DOT_ROWS = 1024
FIRST_HALF_STEP = 2
SECOND_HALF_STEP = 7


def _inproj_kernel(x_hbm, g_ref, w_ref, o_ref, xbuf, h_even, h_odd, sem, *, n_row_tiles, n_col_tiles):
    i = pl.program_id(0)
    j = pl.program_id(1)
    tm = xbuf.shape[0]
    half = tm // 2
    has_next = i + 1 < n_row_tiles
    assert (SECOND_HALF_STEP - FIRST_HALF_STEP) * NORM_CHUNK >= half
    assert (n_col_tiles - SECOND_HALF_STEP) * NORM_CHUNK >= half
    assert NORM_CHUNK % NORM_SUB == 0 and half % NORM_SUB == 0 and NORM_CHUNK <= half

    def x_copy(tile, part):
        return pltpu.make_async_copy(
            x_hbm.at[pl.ds(tile * tm + part * half, half), :],
            xbuf.at[pl.ds(part * half, half), :],
            sem.at[part])

    @pl.when((i == 0) & (j == 0))
    def _():
        x_copy(0, 0).start()
        x_copy(0, 1).start()
        x_copy(0, 0).wait()
        x_copy(0, 1).wait()

        def body(c, carry):
            r = pl.multiple_of(c * NORM_ROWS, NORM_ROWS)
            h_even[pl.ds(r, NORM_ROWS), :] = _rms(xbuf[pl.ds(r, NORM_ROWS), :], g_ref[0]).astype(BF16)
            return carry

        lax.fori_loop(0, tm // NORM_ROWS, body, 0)

    @pl.when(has_next & (j == 0))
    def _():
        x_copy(i + 1, 0).start()

    @pl.when(has_next & (j == FIRST_HALF_STEP))
    def _():
        x_copy(i + 1, 0).wait()

    @pl.when(has_next & (j == SECOND_HALF_STEP - 1))
    def _():
        x_copy(i + 1, 1).start()

    @pl.when(has_next & (j == SECOND_HALF_STEP))
    def _():
        x_copy(i + 1, 1).wait()

    last = half - NORM_CHUNK
    start = jnp.where(
        j < FIRST_HALF_STEP, half + last,
        jnp.where(j < SECOND_HALF_STEP,
                  jnp.minimum((j - FIRST_HALF_STEP) * NORM_CHUNK, last),
                  half + jnp.minimum((j - SECOND_HALF_STEP) * NORM_CHUNK, last)))
    start = pl.multiple_of(start, NORM_SUB)

    def step(h_cur, h_next):
        for k in range(0, NORM_CHUNK, NORM_SUB):
            rows = pl.ds(start + k, NORM_SUB)
            h_next[rows, :] = _rms(xbuf[rows, :], g_ref[0]).astype(BF16)
        dot_rows = min(DOT_ROWS, tm)
        for r in range(0, tm, dot_rows):
            o_ref[r:r + dot_rows, :] = jnp.dot(h_cur[r:r + dot_rows, :], w_ref[0],
                                               preferred_element_type=F32).astype(o_ref.dtype)

    @pl.when(i % 2 == 0)
    def _():
        step(h_even, h_odd)

    @pl.when(i % 2 == 1)
    def _():
        step(h_odd, h_even)


def _inproj_call(xf, norm_g, w_in, layer, *, tm, tn):
    m, d = xf.shape
    n = w_in.shape[-1]
    return pl.pallas_call(
        functools.partial(_inproj_kernel, n_row_tiles=m // tm, n_col_tiles=n // tn),
        grid=(m // tm, n // tn),
        in_specs=[
            pl.BlockSpec(memory_space=pl.ANY),
            pl.BlockSpec((1, 1, d), lambda i, j: (layer, 0, 0)),
            pl.BlockSpec((1, d, tn), lambda i, j: (layer, 0, j)),
        ],
        out_specs=pl.BlockSpec((tm, tn), lambda i, j: (i, j)),
        out_shape=jax.ShapeDtypeStruct((m, n), BF16),
        scratch_shapes=[
            pltpu.VMEM((tm, d), F32),
            pltpu.VMEM((tm, d), BF16),
            pltpu.VMEM((tm, d), BF16),
            pltpu.SemaphoreType.DMA((2,)),
        ],
        compiler_params=_params(2),
        name="in_proj",
    )(xf, norm_g, w_in)


STEP = 256
STEP_GROUPS = STEP // GROUP_ROWS
CARRY_ROWS = 8
MEM_ROWS = 128
MERGE_COLS = 512
GATE_BLOCK = 1024
KEY_LANES = 2 * HEAD_DIM
OUT_COLS = 256
NORM_AHEAD = 2
CONV_ROWS = 128
CONV_FIRST_SLOT = 8

def _mix_merge_kernel(ab_ref, ac_ref, ax_ref, az_ref, q_ref, k_ref, v_ref, bz_ref, mq_ref, mz_ref,
                      mk_ref, mv_ref, bias_ref, cw_ref, qg_ref, kg_ref, mqg_ref,
                      ga0, ga1, gb0, gb1, gm0, gm1, x_ref, bg_ref, wa_ref, wb_ref, wm_ref, wo_ref,
                      o_ref, u_scr, kw_scr, vw_scr, qn_scr, mqn_scr, y_scr, yp_scr, m_scr, *, n_tiles, tiles_per_seq):
    t = STEP
    n_heads_b = W_B // HEAD_DIM
    n_heads_m = W_M // HEAD_DIM
    step = pl.program_id(0)
    tile = jnp.minimum(step, n_tiles - 1)
    seq_start = tile % tiles_per_seq == 0
    scale = HEAD_DIM ** -0.5 * LOG2E
    ones = jnp.ones((HEAD_DIM, HEAD_DIM), BF16)

    @pl.when(step == 0)
    def _():
        one_hot = jnp.where(lax.broadcasted_iota(jnp.int32, (t, HEAD_DIM), 1) == 0, 1.0, 0.0).astype(BF16)
        y_scr[...] = jnp.zeros(y_scr.shape, BF16)
        u_scr[...] = jnp.zeros(u_scr.shape, F32)
        for h in range(n_heads_b):
            kw_scr[h] = jnp.zeros(kw_scr.shape[1:], BF16)
            vw_scr[h] = jnp.zeros(vw_scr.shape[1:], BF16)
            qn_scr[h, :, HEAD_DIM:] = one_hot

    yp_scr[...] = y_scr[...]

    @pl.when(seq_start)
    def _():
        lane = lax.broadcasted_iota(jnp.int32, (HALO, KEY_LANES), 1)
        pen = jnp.where(lane == HEAD_DIM, MASK_VALUE, 0.0).astype(BF16)
        u_scr[t:t + CARRY_ROWS, :] = jnp.zeros((CARRY_ROWS, W_A), F32)
        for h in range(n_heads_b):
            kw_scr[h, t:, :] = pen
            vw_scr[h, t:, :] = jnp.zeros((HALO, KEY_LANES), BF16)

    d = o_ref.shape[1]
    gates = ((ga0, ga1), (gb0, gb1), (gm0, gm1))

    def conv_taps():
        u_scr[0:CARRY_ROWS, :] = u_scr[t:t + CARRY_ROWS, :]
        u_scr[CARRY_ROWS:CARRY_ROWS + t, :] = ac_ref[...].astype(F32) * ax_ref[...].astype(F32)

    def conv_rows(r0):
        rs = slice(r0, r0 + CONV_ROWS)
        y = u_scr[CARRY_ROWS - 2 + r0:CARRY_ROWS - 2 + r0 + CONV_ROWS, :] * cw_ref[0:1, :]
        y = y + u_scr[CARRY_ROWS - 1 + r0:CARRY_ROWS - 1 + r0 + CONV_ROWS, :] * cw_ref[1:2, :]
        y = y + u_scr[CARRY_ROWS + r0:CARRY_ROWS + r0 + CONV_ROWS, :] * cw_ref[2:3, :]
        ya = (ab_ref[rs, :].astype(F32) * y) * _silu(az_ref[rs, :].astype(F32))
        y_scr[rs, 0:W_A] = ya.astype(BF16)

    def band_norm(h):
        hs = slice(h * HEAD_DIM, (h + 1) * HEAD_DIM)
        for r in range(0, HALO, t):
            kw_scr[h, r:r + t, :] = kw_scr[h, r + t:r + 2 * t, :]
            vw_scr[h, r:r + t, :] = vw_scr[h, r + t:r + 2 * t, :]
        kw_scr[h, HALO:, 0:HEAD_DIM] = _head_rms(k_ref[:, hs].astype(F32), kg_ref[...], ones).astype(BF16)
        kw_scr[h, HALO:, HEAD_DIM:] = jnp.zeros((t, HEAD_DIM), BF16)
        vw_scr[h, HALO:, 0:HEAD_DIM] = v_ref[:, hs]
        vw_scr[h, HALO:, HEAD_DIM:] = jnp.ones((t, HEAD_DIM), BF16)
        qn_scr[h, :, 0:HEAD_DIM] = _head_rms(q_ref[:, hs].astype(F32), qg_ref[...] * scale, ones).astype(BF16)

    def band_scores(h):
        ps = []
        for g in range(STEP_GROUPS):
            r0 = g * GROUP_ROWS
            s = lax.dot_general(qn_scr[h, r0:r0 + GROUP_ROWS, :], kw_scr[h, r0:r0 + WINDOW, :],
                                (((1,), (1,)), ((), ())), preferred_element_type=F32)
            s = s + bias_ref[h]
            ps.append(jnp.exp2(s - jnp.max(s, axis=-1, keepdims=True)).astype(BF16))
        return ps

    def band_values(h, ps):
        hs = slice(h * HEAD_DIM, (h + 1) * HEAD_DIM)
        for g, p in enumerate(ps):
            r0 = g * GROUP_ROWS
            rs = slice(r0, r0 + GROUP_ROWS)
            od = jnp.dot(p, vw_scr[h, r0:r0 + WINDOW, :], preferred_element_type=F32)
            o = od[:, 0:HEAD_DIM] / od[:, HEAD_DIM:]
            y_scr[rs, W_A + h * HEAD_DIM:W_A + (h + 1) * HEAD_DIM] = (
                o * _silu(bz_ref[rs, hs].astype(F32))).astype(BF16)

    def mem_norm(h):
        hs = slice(h * HEAD_DIM, (h + 1) * HEAD_DIM)
        mqn_scr[h] = _head_rms(mq_ref[:, hs].astype(F32), mqg_ref[...] * scale, ones).astype(BF16)

    def mem_scores(h):
        mk = mk_ref[0, 0, :, h * HEAD_DIM:(h + 1) * HEAD_DIM]
        ps = []
        for r0 in range(0, t, MEM_ROWS):
            s = lax.dot_general(mqn_scr[h, r0:r0 + MEM_ROWS, :], mk, (((1,), (1,)), ((), ())),
                                preferred_element_type=F32)
            ps.append(jnp.exp2(s - jnp.max(s, axis=-1, keepdims=True)).astype(BF16))
        return ps

    def mem_values(h, ps):
        hs = slice(h * HEAD_DIM, (h + 1) * HEAD_DIM)
        mv = mv_ref[0, 0, h]
        for i, p in enumerate(ps):
            rs = slice(i * MEM_ROWS, (i + 1) * MEM_ROWS)
            od = jnp.dot(p, mv, preferred_element_type=F32)
            o = od[:, 0:HEAD_DIM] / od[:, HEAD_DIM:]
            c = W_A + W_B + h * HEAD_DIM
            y_scr[rs, c:c + HEAD_DIM] = (o * _silu(mz_ref[rs, hs].astype(F32))).astype(BF16)

    def merge(j):
        c = j * MERGE_COLS
        cs = slice(c, c + MERGE_COLS)
        gs = slice(c % GATE_BLOCK, c % GATE_BLOCK + MERGE_COLS)
        blk = c // GATE_BLOCK
        pa = jnp.dot(yp_scr[:, 0:W_A], wa_ref[:, cs], preferred_element_type=F32)
        acc = _sigmoid(gates[0][blk][:, gs].astype(F32) + bg_ref[0:1, cs]) * pa
        pb = jnp.dot(yp_scr[:, W_A:W_A + W_B], wb_ref[:, cs], preferred_element_type=F32)
        acc = acc + _sigmoid(gates[1][blk][:, gs].astype(F32) + bg_ref[1:2, cs]) * pb
        pm = jnp.dot(yp_scr[:, W_A + W_B:], wm_ref[:, cs], preferred_element_type=F32)
        acc = acc + _sigmoid(gates[2][blk][:, gs].astype(F32) + bg_ref[2:3, cs]) * pm
        m_scr[:, cs] = acc.astype(BF16)

    def out(j):
        cs = slice(j * OUT_COLS, (j + 1) * OUT_COLS)
        o_ref[:, cs] = x_ref[:, cs] + jnp.dot(m_scr[...], wo_ref[:, cs], preferred_element_type=F32)

    heads = ([(band_norm, band_scores, band_values, h) for h in range(n_heads_b)]
             + [(mem_norm, mem_scores, mem_values, h) for h in range(n_heads_m)])
    fillers = ([functools.partial(merge, j) for j in range(d // MERGE_COLS)]
               + [functools.partial(out, j) for j in range(d // OUT_COLS)])
    assert len(fillers) == len(heads)
    conv_units = [conv_taps] + [functools.partial(conv_rows, r0) for r0 in range(0, t, CONV_ROWS)]
    fillers[0]()
    for norm, _, _, h in heads[:NORM_AHEAD]:
        norm(h)
    ps_next = heads[0][1](heads[0][3])
    for i, (_, _, values, h) in enumerate(heads):
        ps = ps_next
        if i + NORM_AHEAD < len(heads):
            norm_ahead, _, _, h_ahead = heads[i + NORM_AHEAD]
            norm_ahead(h_ahead)
        if i + 1 < len(fillers):
            fillers[i + 1]()
        if i >= CONV_FIRST_SLOT and conv_units:
            conv_units.pop(0)()
        if i + 1 < len(heads):
            _, scores_next, _, h_next = heads[i + 1]
            ps_next = scores_next(h_next)
        values(h, ps)
    assert not conv_units


def _mix_merge_call(proj, xf, mk, mv, bias, conv_w, q_g, k_g, mq_g, b_gate, w_a, w_b, w_m, w_o,
                    layer, *, seq, cols):
    n_tok, d = xf.shape
    t = STEP
    n_tiles = n_tok // t
    tiles_per_seq = seq // t
    n_heads = W_B // HEAD_DIM
    n_heads_m = W_M // HEAD_DIM
    mem_len = mk.shape[2]

    def mix_tile(s):
        return jnp.minimum(s, n_tiles - 1)

    def merge_tile(s):
        return jnp.maximum(s - 1, 0)

    def mix_spec(width, col_block):
        return pl.BlockSpec((t, width), lambda s: (mix_tile(s), col_block))

    ca, cb, cm = cols["conv"] // W_A, cols["attn"] // W_B, cols["mem"] // W_M
    gblk = cols["gate"] // GATE_BLOCK
    resident = dict(pipeline_mode=pl.Buffered(1))
    gate_specs = [pl.BlockSpec((t, GATE_BLOCK), lambda s, j=j: (merge_tile(s), gblk + j))
                  for j in range(3 * d // GATE_BLOCK)]
    in_specs = [
        mix_spec(W_A, ca), mix_spec(W_A, ca + 1), mix_spec(W_A, ca + 2), mix_spec(W_A, ca + 3),
        mix_spec(W_B, cb), mix_spec(W_B, cb + 1), mix_spec(W_B, cb + 2), mix_spec(W_B, cb + 3),
        mix_spec(W_M, cm), mix_spec(W_M, cm + 1),
        pl.BlockSpec((1, 1, mem_len, W_M), lambda s: (layer, mix_tile(s) // tiles_per_seq, 0, 0)),
        pl.BlockSpec((1, 1, n_heads_m, mem_len, 2 * HEAD_DIM),
                     lambda s: (layer, mix_tile(s) // tiles_per_seq, 0, 0, 0)),
        pl.BlockSpec((None, n_heads, GROUP_ROWS, WINDOW), lambda s: (layer, 0, 0, 0), **resident),
        pl.BlockSpec((None, CONV_WIDTH, W_A), lambda s: (layer, 0, 0)),
        pl.BlockSpec((None, 1, HEAD_DIM), lambda s: (layer, 0, 0)),
        pl.BlockSpec((None, 1, HEAD_DIM), lambda s: (layer, 0, 0)),
        pl.BlockSpec((None, 1, HEAD_DIM), lambda s: (layer, 0, 0)),
    ] + gate_specs + [
        pl.BlockSpec((t, d), lambda s: (merge_tile(s), 0)),
        pl.BlockSpec((None, 3, d), lambda s: (layer, 0, 0)),
        pl.BlockSpec((None, W_A, d), lambda s: (layer, 0, 0), **resident),
        pl.BlockSpec((None, W_B, d), lambda s: (layer, 0, 0), **resident),
        pl.BlockSpec((None, W_M, d), lambda s: (layer, 0, 0), **resident),
        pl.BlockSpec((None, d, d), lambda s: (layer, 0, 0), **resident),
    ]
    return pl.pallas_call(
        functools.partial(_mix_merge_kernel, n_tiles=n_tiles, tiles_per_seq=tiles_per_seq),
        grid=(n_tiles + 1,),
        in_specs=in_specs,
        out_specs=pl.BlockSpec((t, d), lambda s: (merge_tile(s), 0)),
        out_shape=jax.ShapeDtypeStruct((n_tok, d), F32),
        scratch_shapes=[
            pltpu.VMEM((t + CARRY_ROWS, W_A), F32),
            pltpu.VMEM((n_heads, HALO + t, KEY_LANES), BF16),
            pltpu.VMEM((n_heads, HALO + t, 2 * HEAD_DIM), BF16),
            pltpu.VMEM((n_heads, t, KEY_LANES), BF16),
            pltpu.VMEM((n_heads_m, t, HEAD_DIM), BF16),
            pltpu.VMEM((t, W_A + W_B + W_M), BF16),
            pltpu.VMEM((t, W_A + W_B + W_M), BF16),
            pltpu.VMEM((t, d), BF16),
        ],
        compiler_params=_params(1),
        name="mix_merge",
    )(*([proj] * 10), mk, mv, bias, conv_w, q_g, k_g, mq_g, *([proj] * len(gate_specs)),
      xf, b_gate, w_a, w_b, w_m, w_o)


def kernel(x, mem, norm_g, w_in, b_gate, conv_w, q_norm_g, k_norm_g, rel_table, mem_norm_g,
           w_mem_kv, mq_norm_g, mk_norm_g, w_branch_a, w_branch_b, w_branch_m, w_out):
    bsz, seq, d = x.shape
    n_layers = w_in.shape[0]
    n_tok = bsz * seq
    assert seq % HALO == 0 and HALO % STEP == 0 and d == 2 * GATE_BLOCK

    cols = {"conv": 0, "attn": 4 * W_A, "mem": 4 * W_A + 4 * W_B, "gate": 4 * W_A + 4 * W_B + 2 * W_M}
    w_in_b = w_in.astype(BF16)
    w_kv = w_mem_kv.astype(BF16)
    w_a, w_b, w_m, w_o = (w.astype(BF16) for w in (w_branch_a, w_branch_b, w_branch_m, w_out))
    norm_g3 = norm_g.reshape(n_layers, 1, d)
    q_g = q_norm_g.reshape(n_layers, 1, HEAD_DIM)
    k_g = k_norm_g.reshape(n_layers, 1, HEAD_DIM)
    mq_g = mq_norm_g.reshape(n_layers, 1, HEAD_DIM)

    tm_in = min(2048, n_tok)
    tn_in = 1024

    bias = _bias_call(rel_table)
    mk, mv = _memkv_call(mem, mem_norm_g, w_kv, mk_norm_g)

    xf = x.reshape(n_tok, d)
    for layer in range(n_layers):
        proj = _inproj_call(xf, norm_g3, w_in_b, layer, tm=tm_in, tn=tn_in)
        xf = _mix_merge_call(proj, xf, mk, mv, bias, conv_w, q_g, k_g, mq_g, b_gate,
                             w_a, w_b, w_m, w_o, layer, seq=seq, cols=cols)
    return xf.reshape(bsz, seq, d)
```

```python
import functools

import jax
import jax.numpy as jnp
from jax import lax
from jax.experimental import pallas as pl
from jax.experimental.pallas import tpu as pltpu

F32 = jnp.float32
BF16 = jnp.bfloat16

CHUNK = 64
N_PREV = 8
N_BAND = N_PREV + 1
HEAD_DIM = 128
W_A = 512
W_B = 1024
W_M = 512
CONV_WIDTH = 3
MAX_REL = 256
EPS = 1e-6
LOG2E = 1.4426950408889634
MASK_VALUE = -1e30

GROUP_CHUNKS = 2
GROUP_ROWS = GROUP_CHUNKS * CHUNK
WINDOW = (N_BAND + GROUP_CHUNKS - 1) * CHUNK
HALO = N_PREV * CHUNK
BIAS_LANES = 1024

VMEM_LIMIT_BYTES = 58 * 1024 * 1024


def _params(n_axes):
    return pltpu.CompilerParams(
        dimension_semantics=("arbitrary",) * n_axes,
        vmem_limit_bytes=VMEM_LIMIT_BYTES,
    )


def _sigmoid(x):
    return 1.0 / (1.0 + jnp.exp2(x * -LOG2E))


def _silu(x):
    return x * _sigmoid(x)


def _rms(x, g):
    ms = jnp.mean(x * x, axis=-1, keepdims=True)
    return (x * lax.rsqrt(ms + EPS)) * g


def _head_rms(x, g, ones):
    ms = jnp.dot((x * x).astype(BF16), ones, preferred_element_type=F32) * (1.0 / HEAD_DIM)
    return (x * lax.rsqrt(ms + EPS)) * g


def _bias_kernel(table_ref, o_ref, *, n_heads):
    l = pl.program_id(0)
    h = pl.program_id(1)
    n_rel = 2 * MAX_REL + 1
    base = l * (n_rel * n_heads) + h
    lane = lax.broadcasted_iota(jnp.int32, (8, BIAS_LANES), 1)
    rel = N_PREV * CHUNK - (lane - CHUNK)
    idx = jnp.clip(rel, -MAX_REL, MAX_REL) + MAX_REL
    lo = MAX_REL - (CHUNK - 1)

    def fill(r, acc):
        return jnp.where(idx == r, table_ref[base + r * n_heads], acc)

    f_row = lax.fori_loop(lo, n_rel, fill, jnp.zeros((8, BIAS_LANES), F32))
    f_rows = jnp.broadcast_to(f_row[0:1, :] * LOG2E, (CHUNK, BIAS_LANES))
    col = lax.broadcasted_iota(jnp.int32, (CHUNK, WINDOW), 1)
    r0 = pltpu.roll(f_rows, BIAS_LANES - CHUNK, 1, stride=1, stride_axis=0)[:, :WINDOW]
    o_ref[0, 0, 0:CHUNK, :] = jnp.where(col < N_BAND * CHUNK, r0, MASK_VALUE)
    r1 = pltpu.roll(f_rows, 0, 1, stride=1, stride_axis=0)[:, :WINDOW]
    o_ref[0, 0, CHUNK:GROUP_ROWS, :] = jnp.where(col >= CHUNK, r1, MASK_VALUE)


def _bias_call(rel_table):
    n_layers, n_rel, n_heads = rel_table.shape
    return pl.pallas_call(
        functools.partial(_bias_kernel, n_heads=n_heads),
        grid=(n_layers, n_heads),
        in_specs=[pl.BlockSpec(memory_space=pltpu.SMEM)],
        out_specs=pl.BlockSpec((1, 1, GROUP_ROWS, WINDOW), lambda l, h: (l, h, 0, 0)),
        out_shape=jax.ShapeDtypeStruct((n_layers, n_heads, GROUP_ROWS, WINDOW), F32),
        compiler_params=_params(2),
        name="rel_bias",
    )(rel_table.reshape(-1))


def _memkv_kernel(mem_ref, g_ref, w_ref, kg_ref, mk_ref, mv_ref):
    h = _rms(mem_ref[0], g_ref[0]).astype(BF16)
    kv = jnp.dot(h, w_ref[0], preferred_element_type=F32)
    for hd in range(W_M // HEAD_DIM):
        cs = slice(hd * HEAD_DIM, (hd + 1) * HEAD_DIM)
        mk_ref[0, 0, :, cs] = _rms(kv[:, cs], kg_ref[0]).astype(BF16)
        mv_ref[0, 0, hd, :, 0:HEAD_DIM] = kv[:, W_M + hd * HEAD_DIM:W_M + (hd + 1) * HEAD_DIM].astype(BF16)
        mv_ref[0, 0, hd, :, HEAD_DIM:] = jnp.ones((kv.shape[0], HEAD_DIM), BF16)


def _memkv_call(mem, mem_norm_g, w_mem_kv, mk_norm_g):
    bsz, mem_len, d = mem.shape
    n_layers = w_mem_kv.shape[0]
    n_heads = W_M // HEAD_DIM
    return pl.pallas_call(
        _memkv_kernel,
        grid=(n_layers, bsz),
        in_specs=[
            pl.BlockSpec((1, mem_len, d), lambda l, b: (b, 0, 0)),
            pl.BlockSpec((1, 1, d), lambda l, b: (l, 0, 0)),
            pl.BlockSpec((1, d, 2 * W_M), lambda l, b: (l, 0, 0)),
            pl.BlockSpec((1, 1, HEAD_DIM), lambda l, b: (l, 0, 0)),
        ],
        out_specs=[
            pl.BlockSpec((1, 1, mem_len, W_M), lambda l, b: (l, b, 0, 0)),
            pl.BlockSpec((1, 1, n_heads, mem_len, 2 * HEAD_DIM), lambda l, b: (l, b, 0, 0, 0)),
        ],
        out_shape=[
            jax.ShapeDtypeStruct((n_layers, bsz, mem_len, W_M), BF16),
            jax.ShapeDtypeStruct((n_layers, bsz, n_heads, mem_len, 2 * HEAD_DIM), BF16),
        ],
        compiler_params=_params(2),
        name="mem_kv",
    )(mem, mem_norm_g.reshape(n_layers, 1, d), w_mem_kv, mk_norm_g.reshape(n_layers, 1, HEAD_DIM))


NORM_ROWS = 128
NORM_CHUNK = 208
NORM_SUB = 16
DOT_ROWS = 1024
FIRST_HALF_STEP = 2
SECOND_HALF_STEP = 7


def _inproj_kernel(x_hbm, g_ref, w_ref, o_ref, xbuf, h_even, h_odd, sem, *, n_row_tiles, n_col_tiles):
    i = pl.program_id(0)
    j = pl.program_id(1)
    tm = xbuf.shape[0]
    half = tm // 2
    has_next = i + 1 < n_row_tiles
    assert (SECOND_HALF_STEP - FIRST_HALF_STEP) * NORM_CHUNK >= half
    assert (n_col_tiles - SECOND_HALF_STEP) * NORM_CHUNK >= half
    assert NORM_CHUNK % NORM_SUB == 0 and half % NORM_SUB == 0 and NORM_CHUNK <= half

    def x_copy(tile, part):
        return pltpu.make_async_copy(
            x_hbm.at[pl.ds(tile * tm + part * half, half), :],
            xbuf.at[pl.ds(part * half, half), :],
            sem.at[part])

    @pl.when((i == 0) & (j == 0))
    def _():
        x_copy(0, 0).start()
        x_copy(0, 1).start()
        x_copy(0, 0).wait()
        x_copy(0, 1).wait()

        def body(c, carry):
            r = pl.multiple_of(c * NORM_ROWS, NORM_ROWS)
            h_even[pl.ds(r, NORM_ROWS), :] = _rms(xbuf[pl.ds(r, NORM_ROWS), :], g_ref[0]).astype(BF16)
            return carry

        lax.fori_loop(0, tm // NORM_ROWS, body, 0)

    @pl.when(has_next & (j == 0))
    def _():
        x_copy(i + 1, 0).start()

    @pl.when(has_next & (j == FIRST_HALF_STEP))
    def _():
        x_copy(i + 1, 0).wait()

    @pl.when(has_next & (j == SECOND_HALF_STEP - 1))
    def _():
        x_copy(i + 1, 1).start()

    @pl.when(has_next & (j == SECOND_HALF_STEP))
    def _():
        x_copy(i + 1, 1).wait()

    last = half - NORM_CHUNK
    start = jnp.where(
        j < FIRST_HALF_STEP, half + last,
        jnp.where(j < SECOND_HALF_STEP,
                  jnp.minimum((j - FIRST_HALF_STEP) * NORM_CHUNK, last),
                  half + jnp.minimum((j - SECOND_HALF_STEP) * NORM_CHUNK, last)))
    start = pl.multiple_of(start, NORM_SUB)

    def step(h_cur, h_next):
        for k in range(0, NORM_CHUNK, NORM_SUB):
            rows = pl.ds(start + k, NORM_SUB)
            h_next[rows, :] = _rms(xbuf[rows, :], g_ref[0]).astype(BF16)
        dot_rows = min(DOT_ROWS, tm)
        for r in range(0, tm, dot_rows):
            o_ref[r:r + dot_rows, :] = jnp.dot(h_cur[r:r + dot_rows, :], w_ref[0],
                                               preferred_element_type=F32).astype(o_ref.dtype)

    @pl.when(i % 2 == 0)
    def _():
        step(h_even, h_odd)

    @pl.when(i % 2 == 1)
    def _():
        step(h_odd, h_even)


def _inproj_call(xf, norm_g, w_in, layer, *, tm, tn, col_shift):
    m, d = xf.shape
    n = w_in.shape[-1]
    n_col = n // tn
    return pl.pallas_call(
        functools.partial(_inproj_kernel, n_row_tiles=m // tm, n_col_tiles=n // tn),
        grid=(m // tm, n // tn),
        in_specs=[
            pl.BlockSpec(memory_space=pl.ANY),
            pl.BlockSpec((1, 1, d), lambda i, j: (layer, 0, 0)),
            pl.BlockSpec((1, d, tn), lambda i, j: (layer, 0, j)),
        ],
        out_specs=pl.BlockSpec((tm, tn), lambda i, j: (i, (j + col_shift) % n_col)),
        out_shape=jax.ShapeDtypeStruct((m, n), BF16),
        scratch_shapes=[
            pltpu.VMEM((tm, d), F32),
            pltpu.VMEM((tm, d), BF16),
            pltpu.VMEM((tm, d), BF16),
            pltpu.SemaphoreType.DMA((2,)),
        ],
        compiler_params=_params(2),
        name="in_proj",
    )(xf, norm_g, w_in)


STEP = 256
STEP_GROUPS = STEP // GROUP_ROWS
CARRY_ROWS = 8
MEM_ROWS = 128
MERGE_COLS = 512
GATE_BLOCK = 1024
KEY_LANES = 2 * HEAD_DIM
OUT_COLS = 256
NORM_AHEAD = 2
CONV_ROWS = 128
CONV_FIRST_SLOT = 8

def _mix_merge_kernel(conv_ref, attn_ref, memq_ref, mk_ref, mv_ref, bias_ref, cw_ref, qg_ref, kg_ref, mqg_ref,
                      gate_ref, x_ref, bg_ref, wa_ref, wb_ref, wm_ref, wo_ref,
                      o_ref, u_scr, kw_scr, vw_scr, qn_scr, mqn_scr, y_scr, yp_scr, m_scr, *, n_tiles, tiles_per_seq):
    ab_ref, ac_ref, ax_ref, az_ref = (conv_ref.at[:, j * W_A:(j + 1) * W_A] for j in range(4))
    q_ref, k_ref, v_ref, bz_ref = (attn_ref.at[:, j * W_B:(j + 1) * W_B] for j in range(4))
    mq_ref, mz_ref = (memq_ref.at[:, j * W_M:(j + 1) * W_M] for j in range(2))
    ga0, ga1, gb0, gb1, gm0, gm1 = (gate_ref.at[:, j * GATE_BLOCK:(j + 1) * GATE_BLOCK] for j in range(6))
    t = STEP
    n_heads_b = W_B // HEAD_DIM
    n_heads_m = W_M // HEAD_DIM
    step = pl.program_id(0)
    tile = jnp.minimum(step, n_tiles - 1)
    seq_start = tile % tiles_per_seq == 0
    scale = HEAD_DIM ** -0.5 * LOG2E
    ones = jnp.ones((HEAD_DIM, HEAD_DIM), BF16)

    @pl.when(step == 0)
    def _():
        one_hot = jnp.where(lax.broadcasted_iota(jnp.int32, (t, HEAD_DIM), 1) == 0, 1.0, 0.0).astype(BF16)
        y_scr[...] = jnp.zeros(y_scr.shape, BF16)
        u_scr[...] = jnp.zeros(u_scr.shape, F32)
        for h in range(n_heads_b):
            kw_scr[h] = jnp.zeros(kw_scr.shape[1:], BF16)
            vw_scr[h] = jnp.zeros(vw_scr.shape[1:], BF16)
            qn_scr[h, :, HEAD_DIM:] = one_hot

    yp_scr[...] = y_scr[...]

    @pl.when(seq_start)
    def _():
        lane = lax.broadcasted_iota(jnp.int32, (HALO, KEY_LANES), 1)
        pen = jnp.where(lane == HEAD_DIM, MASK_VALUE, 0.0).astype(BF16)
        u_scr[t:t + CARRY_ROWS, :] = jnp.zeros((CARRY_ROWS, W_A), F32)
        for h in range(n_heads_b):
            kw_scr[h, t:, :] = pen
            vw_scr[h, t:, :] = jnp.zeros((HALO, KEY_LANES), BF16)

    d = o_ref.shape[1]
    gates = ((ga0, ga1), (gb0, gb1), (gm0, gm1))

    def conv_taps():
        u_scr[0:CARRY_ROWS, :] = u_scr[t:t + CARRY_ROWS, :]
        u_scr[CARRY_ROWS:CARRY_ROWS + t, :] = ac_ref[...].astype(F32) * ax_ref[...].astype(F32)

    def conv_rows(r0):
        rs = slice(r0, r0 + CONV_ROWS)
        y = u_scr[CARRY_ROWS - 2 + r0:CARRY_ROWS - 2 + r0 + CONV_ROWS, :] * cw_ref[0:1, :]
        y = y + u_scr[CARRY_ROWS - 1 + r0:CARRY_ROWS - 1 + r0 + CONV_ROWS, :] * cw_ref[1:2, :]
        y = y + u_scr[CARRY_ROWS + r0:CARRY_ROWS + r0 + CONV_ROWS, :] * cw_ref[2:3, :]
        ya = (ab_ref[rs, :].astype(F32) * y) * _silu(az_ref[rs, :].astype(F32))
        y_scr[rs, 0:W_A] = ya.astype(BF16)

    def band_norm(h):
        hs = slice(h * HEAD_DIM, (h + 1) * HEAD_DIM)
        for r in range(0, HALO, t):
            kw_scr[h, r:r + t, :] = kw_scr[h, r + t:r + 2 * t, :]
            vw_scr[h, r:r + t, :] = vw_scr[h, r + t:r + 2 * t, :]
        kw_scr[h, HALO:, 0:HEAD_DIM] = _head_rms(k_ref[:, hs].astype(F32), kg_ref[...], ones).astype(BF16)
        kw_scr[h, HALO:, HEAD_DIM:] = jnp.zeros((t, HEAD_DIM), BF16)
        vw_scr[h, HALO:, 0:HEAD_DIM] = v_ref[:, hs]
        vw_scr[h, HALO:, HEAD_DIM:] = jnp.ones((t, HEAD_DIM), BF16)
        qn_scr[h, :, 0:HEAD_DIM] = _head_rms(q_ref[:, hs].astype(F32), qg_ref[...] * scale, ones).astype(BF16)

    def band_scores(h):
        ps = []
        for g in range(STEP_GROUPS):
            r0 = g * GROUP_ROWS
            s = lax.dot_general(qn_scr[h, r0:r0 + GROUP_ROWS, :], kw_scr[h, r0:r0 + WINDOW, :],
                                (((1,), (1,)), ((), ())), preferred_element_type=F32)
            s = s + bias_ref[h]
            ps.append(jnp.exp2(s - jnp.max(s, axis=-1, keepdims=True)).astype(BF16))
        return ps

    def band_values(h, ps):
        hs = slice(h * HEAD_DIM, (h + 1) * HEAD_DIM)
        for g, p in enumerate(ps):
            r0 = g * GROUP_ROWS
            rs = slice(r0, r0 + GROUP_ROWS)
            od = jnp.dot(p, vw_scr[h, r0:r0 + WINDOW, :], preferred_element_type=F32)
            o = od[:, 0:HEAD_DIM] / od[:, HEAD_DIM:]
            y_scr[rs, W_A + h * HEAD_DIM:W_A + (h + 1) * HEAD_DIM] = (
                o * _silu(bz_ref[rs, hs].astype(F32))).astype(BF16)

    def mem_norm(h):
        hs = slice(h * HEAD_DIM, (h + 1) * HEAD_DIM)
        mqn_scr[h] = _head_rms(mq_ref[:, hs].astype(F32), mqg_ref[...] * scale, ones).astype(BF16)

    def mem_scores(h):
        mk = mk_ref[0, 0, :, h * HEAD_DIM:(h + 1) * HEAD_DIM]
        ps = []
        for r0 in range(0, t, MEM_ROWS):
            s = lax.dot_general(mqn_scr[h, r0:r0 + MEM_ROWS, :], mk, (((1,), (1,)), ((), ())),
                                preferred_element_type=F32)
            ps.append(jnp.exp2(s - jnp.max(s, axis=-1, keepdims=True)).astype(BF16))
        return ps

    def mem_values(h, ps):
        hs = slice(h * HEAD_DIM, (h + 1) * HEAD_DIM)
        mv = mv_ref[0, 0, h]
        for i, p in enumerate(ps):
            rs = slice(i * MEM_ROWS, (i + 1) * MEM_ROWS)
            od = jnp.dot(p, mv, preferred_element_type=F32)
            o = od[:, 0:HEAD_DIM] / od[:, HEAD_DIM:]
            c = W_A + W_B + h * HEAD_DIM
            y_scr[rs, c:c + HEAD_DIM] = (o * _silu(mz_ref[rs, hs].astype(F32))).astype(BF16)

    def merge(j):
        c = j * MERGE_COLS
        cs = slice(c, c + MERGE_COLS)
        gs = slice(c % GATE_BLOCK, c % GATE_BLOCK + MERGE_COLS)
        blk = c // GATE_BLOCK
        pa = jnp.dot(yp_scr[:, 0:W_A], wa_ref[:, cs], preferred_element_type=F32)
        acc = _sigmoid(gates[0][blk][:, gs].astype(F32) + bg_ref[0:1, cs]) * pa
        pb = jnp.dot(yp_scr[:, W_A:W_A + W_B], wb_ref[:, cs], preferred_element_type=F32)
        acc = acc + _sigmoid(gates[1][blk][:, gs].astype(F32) + bg_ref[1:2, cs]) * pb
        pm = jnp.dot(yp_scr[:, W_A + W_B:], wm_ref[:, cs], preferred_element_type=F32)
        acc = acc + _sigmoid(gates[2][blk][:, gs].astype(F32) + bg_ref[2:3, cs]) * pm
        m_scr[:, cs] = acc.astype(BF16)

    def out(j):
        cs = slice(j * OUT_COLS, (j + 1) * OUT_COLS)
        o_ref[:, cs] = x_ref[:, cs] + jnp.dot(m_scr[...], wo_ref[:, cs], preferred_element_type=F32)

    heads = ([(band_norm, band_scores, band_values, h) for h in range(n_heads_b)]
             + [(mem_norm, mem_scores, mem_values, h) for h in range(n_heads_m)])
    fillers = ([functools.partial(merge, j) for j in range(d // MERGE_COLS)]
               + [functools.partial(out, j) for j in range(d // OUT_COLS)])
    assert len(fillers) == len(heads)
    conv_units = [conv_taps] + [functools.partial(conv_rows, r0) for r0 in range(0, t, CONV_ROWS)]
    fillers[0]()
    for norm, _, _, h in heads[:NORM_AHEAD]:
        norm(h)
    ps_next = heads[0][1](heads[0][3])
    for i, (_, _, values, h) in enumerate(heads):
        ps = ps_next
        if i + NORM_AHEAD < len(heads):
            norm_ahead, _, _, h_ahead = heads[i + NORM_AHEAD]
            norm_ahead(h_ahead)
        if i + 1 < len(fillers):
            fillers[i + 1]()
        if i >= CONV_FIRST_SLOT and conv_units:
            conv_units.pop(0)()
        if i + 1 < len(heads):
            _, scores_next, _, h_next = heads[i + 1]
            ps_next = scores_next(h_next)
        values(h, ps)
    assert not conv_units


def _mix_merge_call(proj, xf, mk, mv, bias, conv_w, q_g, k_g, mq_g, b_gate, w_a, w_b, w_m, w_o,
                    layer, *, seq, cols):
    n_tok, d = xf.shape
    t = STEP
    n_tiles = n_tok // t
    tiles_per_seq = seq // t
    n_heads = W_B // HEAD_DIM
    n_heads_m = W_M // HEAD_DIM
    mem_len = mk.shape[2]

    def mix_tile(s):
        return jnp.minimum(s, n_tiles - 1)

    def merge_tile(s):
        return jnp.maximum(s - 1, 0)

    def mix_spec(width, col):
        assert col % width == 0
        return pl.BlockSpec((t, width), lambda s: (mix_tile(s), col // width))

    assert cols["gate"] == 0
    resident = dict(pipeline_mode=pl.Buffered(1))
    in_specs = [
        mix_spec(4 * W_A, cols["conv"]), mix_spec(4 * W_B, cols["attn"]), mix_spec(2 * W_M, cols["mem"]),
        pl.BlockSpec((1, 1, mem_len, W_M), lambda s: (layer, mix_tile(s) // tiles_per_seq, 0, 0)),
        pl.BlockSpec((1, 1, n_heads_m, mem_len, 2 * HEAD_DIM),
                     lambda s: (layer, mix_tile(s) // tiles_per_seq, 0, 0, 0)),
        pl.BlockSpec((None, n_heads, GROUP_ROWS, WINDOW), lambda s: (layer, 0, 0, 0), **resident),
        pl.BlockSpec((None, CONV_WIDTH, W_A), lambda s: (layer, 0, 0)),
        pl.BlockSpec((None, 1, HEAD_DIM), lambda s: (layer, 0, 0)),
        pl.BlockSpec((None, 1, HEAD_DIM), lambda s: (layer, 0, 0)),
        pl.BlockSpec((None, 1, HEAD_DIM), lambda s: (layer, 0, 0)),
        pl.BlockSpec((t, 3 * d), lambda s: (merge_tile(s), 0)),
        pl.BlockSpec((t, d), lambda s: (merge_tile(s), 0)),
        pl.BlockSpec((None, 3, d), lambda s: (layer, 0, 0)),
        pl.BlockSpec((None, W_A, d), lambda s: (layer, 0, 0), **resident),
        pl.BlockSpec((None, W_B, d), lambda s: (layer, 0, 0), **resident),
        pl.BlockSpec((None, W_M, d), lambda s: (layer, 0, 0), **resident),
        pl.BlockSpec((None, d, d), lambda s: (layer, 0, 0), **resident),
    ]
    return pl.pallas_call(
        functools.partial(_mix_merge_kernel, n_tiles=n_tiles, tiles_per_seq=tiles_per_seq),
        grid=(n_tiles + 1,),
        in_specs=in_specs,
        out_specs=pl.BlockSpec((t, d), lambda s: (merge_tile(s), 0)),
        out_shape=jax.ShapeDtypeStruct((n_tok, d), F32),
        scratch_shapes=[
            pltpu.VMEM((t + CARRY_ROWS, W_A), F32),
            pltpu.VMEM((n_heads, HALO + t, KEY_LANES), BF16),
            pltpu.VMEM((n_heads, HALO + t, 2 * HEAD_DIM), BF16),
            pltpu.VMEM((n_heads, t, KEY_LANES), BF16),
            pltpu.VMEM((n_heads_m, t, HEAD_DIM), BF16),
            pltpu.VMEM((t, W_A + W_B + W_M), BF16),
            pltpu.VMEM((t, W_A + W_B + W_M), BF16),
            pltpu.VMEM((t, d), BF16),
        ],
        compiler_params=_params(1),
        name="mix_merge",
    )(proj, proj, proj, mk, mv, bias, conv_w, q_g, k_g, mq_g, proj, xf, b_gate, w_a, w_b, w_m, w_o)


def kernel(x, mem, norm_g, w_in, b_gate, conv_w, q_norm_g, k_norm_g, rel_table, mem_norm_g,
           w_mem_kv, mq_norm_g, mk_norm_g, w_branch_a, w_branch_b, w_branch_m, w_out):
    bsz, seq, d = x.shape
    n_layers = w_in.shape[0]
    n_tok = bsz * seq
    assert seq % HALO == 0 and HALO % STEP == 0 and d == 2 * GATE_BLOCK

    n_mix = 4 * W_A + 4 * W_B + 2 * W_M
    n_gate = 3 * d
    cols = {"gate": 0, "conv": n_gate, "attn": n_gate + 4 * W_A, "mem": n_gate + 4 * W_A + 4 * W_B}
    w_in_b = w_in.astype(BF16)
    w_kv = w_mem_kv.astype(BF16)
    w_a, w_b, w_m, w_o = (w.astype(BF16) for w in (w_branch_a, w_branch_b, w_branch_m, w_out))
    norm_g3 = norm_g.reshape(n_layers, 1, d)
    q_g = q_norm_g.reshape(n_layers, 1, HEAD_DIM)
    k_g = k_norm_g.reshape(n_layers, 1, HEAD_DIM)
    mq_g = mq_norm_g.reshape(n_layers, 1, HEAD_DIM)

    tm_in = min(2048, n_tok)
    tn_in = 1024

    bias = _bias_call(rel_table)
    mk, mv = _memkv_call(mem, mem_norm_g, w_kv, mk_norm_g)

    xf = x.reshape(n_tok, d)
    for layer in range(n_layers):
        proj = _inproj_call(xf, norm_g3, w_in_b, layer, tm=tm_in, tn=tn_in, col_shift=n_gate // tn_in)
        xf = _mix_merge_call(proj, xf, mk, mv, bias, conv_w, q_g, k_g, mq_g, b_gate,
                             w_a, w_b, w_m, w_o, layer, seq=seq, cols=cols)
    return xf.reshape(bsz, seq, d)
```

```python
import functools

import jax
import jax.numpy as jnp
from jax import lax
from jax.experimental import pallas as pl
from jax.experimental.pallas import tpu as pltpu

F32 = jnp.float32
BF16 = jnp.bfloat16

CHUNK = 64
N_PREV = 8
N_BAND = N_PREV + 1
HEAD_DIM = 128
W_A = 512
W_B = 1024
W_M = 512
CONV_WIDTH = 3
MAX_REL = 256
EPS = 1e-6
LOG2E = 1.4426950408889634
MASK_VALUE = -1e30

GROUP_CHUNKS = 2
GROUP_ROWS = GROUP_CHUNKS * CHUNK
WINDOW = (N_BAND + GROUP_CHUNKS - 1) * CHUNK
HALO = N_PREV * CHUNK
BIAS_LANES = 1024

VMEM_LIMIT_BYTES = 58 * 1024 * 1024


def _params(n_axes):
    return pltpu.CompilerParams(
        dimension_semantics=("arbitrary",) * n_axes,
        vmem_limit_bytes=VMEM_LIMIT_BYTES,
    )


def _sigmoid(x):
    return 1.0 / (1.0 + jnp.exp2(x * -LOG2E))


def _silu(x):
    return x * _sigmoid(x)


def _rms(x, g):
    ms = jnp.mean(x * x, axis=-1, keepdims=True)
    return (x * lax.rsqrt(ms + EPS)) * g


def _bias_kernel(table_ref, o_ref, *, n_heads):
    l = pl.program_id(0)
    h = pl.program_id(1)
    n_rel = 2 * MAX_REL + 1
    base = l * (n_rel * n_heads) + h
    lane = lax.broadcasted_iota(jnp.int32, (8, BIAS_LANES), 1)
    rel = N_PREV * CHUNK - (lane - CHUNK)
    idx = jnp.clip(rel, -MAX_REL, MAX_REL) + MAX_REL
    lo = MAX_REL - (CHUNK - 1)

    def fill(r, acc):
        return jnp.where(idx == r, table_ref[base + r * n_heads], acc)

    f_row = lax.fori_loop(lo, n_rel, fill, jnp.zeros((8, BIAS_LANES), F32))
    f_rows = jnp.broadcast_to(f_row[0:1, :] * LOG2E, (CHUNK, BIAS_LANES))
    col = lax.broadcasted_iota(jnp.int32, (CHUNK, WINDOW), 1)
    r0 = pltpu.roll(f_rows, BIAS_LANES - CHUNK, 1, stride=1, stride_axis=0)[:, :WINDOW]
    o_ref[0, 0, 0:CHUNK, :] = jnp.where(col < N_BAND * CHUNK, r0, MASK_VALUE)
    r1 = pltpu.roll(f_rows, 0, 1, stride=1, stride_axis=0)[:, :WINDOW]
    o_ref[0, 0, CHUNK:GROUP_ROWS, :] = jnp.where(col >= CHUNK, r1, MASK_VALUE)


def _bias_call(rel_table):
    n_layers, n_rel, n_heads = rel_table.shape
    return pl.pallas_call(
        functools.partial(_bias_kernel, n_heads=n_heads),
        grid=(n_layers, n_heads),
        in_specs=[pl.BlockSpec(memory_space=pltpu.SMEM)],
        out_specs=pl.BlockSpec((1, 1, GROUP_ROWS, WINDOW), lambda l, h: (l, h, 0, 0)),
        out_shape=jax.ShapeDtypeStruct((n_layers, n_heads, GROUP_ROWS, WINDOW), F32),
        compiler_params=_params(2),
        name="rel_bias",
    )(rel_table.reshape(-1))


def _memkv_kernel(mem_ref, g_ref, w_ref, kg_ref, mk_ref, mv_ref):
    h = _rms(mem_ref[0], g_ref[0]).astype(BF16)
    kv = jnp.dot(h, w_ref[0], preferred_element_type=F32)
    for hd in range(W_M // HEAD_DIM):
        cs = slice(hd * HEAD_DIM, (hd + 1) * HEAD_DIM)
        mk_ref[0, 0, :, cs] = _rms(kv[:, cs], kg_ref[0]).astype(BF16)
        mv_ref[0, 0, hd, :, 0:HEAD_DIM] = kv[:, W_M + hd * HEAD_DIM:W_M + (hd + 1) * HEAD_DIM].astype(BF16)
        mv_ref[0, 0, hd, :, HEAD_DIM:] = jnp.ones((kv.shape[0], HEAD_DIM), BF16)


def _memkv_call(mem, mem_norm_g, w_mem_kv, mk_norm_g):
    bsz, mem_len, d = mem.shape
    n_layers = w_mem_kv.shape[0]
    n_heads = W_M // HEAD_DIM
    return pl.pallas_call(
        _memkv_kernel,
        grid=(n_layers, bsz),
        in_specs=[
            pl.BlockSpec((1, mem_len, d), lambda l, b: (b, 0, 0)),
            pl.BlockSpec((1, 1, d), lambda l, b: (l, 0, 0)),
            pl.BlockSpec((1, d, 2 * W_M), lambda l, b: (l, 0, 0)),
            pl.BlockSpec((1, 1, HEAD_DIM), lambda l, b: (l, 0, 0)),
        ],
        out_specs=[
            pl.BlockSpec((1, 1, mem_len, W_M), lambda l, b: (l, b, 0, 0)),
            pl.BlockSpec((1, 1, n_heads, mem_len, 2 * HEAD_DIM), lambda l, b: (l, b, 0, 0, 0)),
        ],
        out_shape=[
            jax.ShapeDtypeStruct((n_layers, bsz, mem_len, W_M), BF16),
            jax.ShapeDtypeStruct((n_layers, bsz, n_heads, mem_len, 2 * HEAD_DIM), BF16),
        ],
        compiler_params=_params(2),
        name="mem_kv",
    )(mem, mem_norm_g.reshape(n_layers, 1, d), w_mem_kv, mk_norm_g.reshape(n_layers, 1, HEAD_DIM))


NORM_ROWS = 128
NORM_CHUNK = 208
NORM_SUB = 16
DOT_ROWS = 1024
FIRST_HALF_STEP = 2
SECOND_HALF_STEP = 7


def _inproj_kernel(x_hbm, g_ref, w_ref, o_ref, xbuf, h_even, h_odd, sem, *, n_row_tiles, n_col_tiles):
    i = pl.program_id(0)
    j = pl.program_id(1)
    tm = xbuf.shape[0]
    half = tm // 2
    has_next = i + 1 < n_row_tiles
    assert (SECOND_HALF_STEP - FIRST_HALF_STEP) * NORM_CHUNK >= half
    assert (n_col_tiles - SECOND_HALF_STEP) * NORM_CHUNK >= half
    assert NORM_CHUNK % NORM_SUB == 0 and half % NORM_SUB == 0 and NORM_CHUNK <= half

    def x_copy(tile, part):
        return pltpu.make_async_copy(
            x_hbm.at[pl.ds(tile * tm + part * half, half), :],
            xbuf.at[pl.ds(part * half, half), :],
            sem.at[part])

    @pl.when((i == 0) & (j == 0))
    def _():
        x_copy(0, 0).start()
        x_copy(0, 1).start()
        x_copy(0, 0).wait()
        x_copy(0, 1).wait()

        def body(c, carry):
            r = pl.multiple_of(c * NORM_ROWS, NORM_ROWS)
            h_even[pl.ds(r, NORM_ROWS), :] = _rms(xbuf[pl.ds(r, NORM_ROWS), :], g_ref[0]).astype(BF16)
            return carry

        lax.fori_loop(0, tm // NORM_ROWS, body, 0)

    @pl.when(has_next & (j == 0))
    def _():
        x_copy(i + 1, 0).start()

    @pl.when(has_next & (j == FIRST_HALF_STEP))
    def _():
        x_copy(i + 1, 0).wait()

    @pl.when(has_next & (j == SECOND_HALF_STEP - 1))
    def _():
        x_copy(i + 1, 1).start()

    @pl.when(has_next & (j == SECOND_HALF_STEP))
    def _():
        x_copy(i + 1, 1).wait()

    last = half - NORM_CHUNK
    start = jnp.where(
        j < FIRST_HALF_STEP, half + last,
        jnp.where(j < SECOND_HALF_STEP,
                  jnp.minimum((j - FIRST_HALF_STEP) * NORM_CHUNK, last),
                  half + jnp.minimum((j - SECOND_HALF_STEP) * NORM_CHUNK, last)))
    start = pl.multiple_of(start, NORM_SUB)

    def step(h_cur, h_next):
        for k in range(0, NORM_CHUNK, NORM_SUB):
            rows = pl.ds(start + k, NORM_SUB)
            h_next[rows, :] = _rms(xbuf[rows, :], g_ref[0]).astype(BF16)
        dot_rows = min(DOT_ROWS, tm)
        for r in range(0, tm, dot_rows):
            o_ref[r:r + dot_rows, :] = jnp.dot(h_cur[r:r + dot_rows, :], w_ref[0],
                                               preferred_element_type=F32).astype(o_ref.dtype)

    @pl.when(i % 2 == 0)
    def _():
        step(h_even, h_odd)

    @pl.when(i % 2 == 1)
    def _():
        step(h_odd, h_even)


def _inproj_call(xf, norm_g, w_in, layer, *, tm, tn, col_shift):
    m, d = xf.shape
    n = w_in.shape[-1]
    n_col = n // tn
    return pl.pallas_call(
        functools.partial(_inproj_kernel, n_row_tiles=m // tm, n_col_tiles=n // tn),
        grid=(m // tm, n // tn),
        in_specs=[
            pl.BlockSpec(memory_space=pl.ANY),
            pl.BlockSpec((1, 1, d), lambda i, j: (layer, 0, 0)),
            pl.BlockSpec((1, d, tn), lambda i, j: (layer, 0, j)),
        ],
        out_specs=pl.BlockSpec((tm, tn), lambda i, j: (i, (j + col_shift) % n_col)),
        out_shape=jax.ShapeDtypeStruct((m, n), BF16),
        scratch_shapes=[
            pltpu.VMEM((tm, d), F32),
            pltpu.VMEM((tm, d), BF16),
            pltpu.VMEM((tm, d), BF16),
            pltpu.SemaphoreType.DMA((2,)),
        ],
        compiler_params=_params(2),
        name="in_proj",
    )(xf, norm_g, w_in)


STEP = 256
STEP_GROUPS = STEP // GROUP_ROWS
CARRY_ROWS = 8
MEM_ROWS = 128
MERGE_COLS = 512
GATE_BLOCK = 1024
KEY_LANES = 2 * HEAD_DIM
OUT_COLS = 256
NORM_AHEAD = 2
MERGE_DRAIN_SLOT = 3
CONV_ROWS = 128
CONV_FIRST_SLOT = 8

def _mix_merge_kernel(conv_ref, attn_ref, memq_ref, mk_ref, mv_ref, bias_ref, cw_ref, qg_ref, kg_ref, mqg_ref,
                      gate_ref, x_ref, bg_ref, wa_ref, wb_ref, wm_ref, wo_ref,
                      o_ref, u_scr, kw_scr, vw_scr, qn_scr, mqn_scr, y_scr, yp_scr, m_scr, *, n_tiles, tiles_per_seq):
    ab_ref, ac_ref, ax_ref, az_ref = (conv_ref.at[:, j * W_A:(j + 1) * W_A] for j in range(4))
    q_ref, k_ref, v_ref, bz_ref = (attn_ref.at[:, j * W_B:(j + 1) * W_B] for j in range(4))
    mq_ref, mz_ref = (memq_ref.at[:, j * W_M:(j + 1) * W_M] for j in range(2))
    ga0, ga1, gb0, gb1, gm0, gm1 = (gate_ref.at[:, j * GATE_BLOCK:(j + 1) * GATE_BLOCK] for j in range(6))
    t = STEP
    n_heads_b = W_B // HEAD_DIM
    n_heads_m = W_M // HEAD_DIM
    step = pl.program_id(0)
    tile = jnp.minimum(step, n_tiles - 1)
    seq_start = tile % tiles_per_seq == 0
    scale = HEAD_DIM ** -0.5 * LOG2E

    @pl.when(step == 0)
    def _():
        one_hot = jnp.where(lax.broadcasted_iota(jnp.int32, (t, HEAD_DIM), 1) == 0, 1.0, 0.0).astype(BF16)
        y_scr[...] = jnp.zeros(y_scr.shape, BF16)
        u_scr[...] = jnp.zeros(u_scr.shape, F32)
        for h in range(n_heads_b):
            kw_scr[h] = jnp.zeros(kw_scr.shape[1:], BF16)
            vw_scr[h] = jnp.zeros(vw_scr.shape[1:], BF16)
            qn_scr[h, :, HEAD_DIM:] = one_hot

    yp_scr[...] = y_scr[...]

    @pl.when(seq_start)
    def _():
        lane = lax.broadcasted_iota(jnp.int32, (HALO, KEY_LANES), 1)
        pen = jnp.where(lane == HEAD_DIM, MASK_VALUE, 0.0).astype(BF16)
        u_scr[t:t + CARRY_ROWS, :] = jnp.zeros((CARRY_ROWS, W_A), F32)
        for h in range(n_heads_b):
            kw_scr[h, t:, :] = pen
            vw_scr[h, t:, :] = jnp.zeros((HALO, KEY_LANES), BF16)

    d = o_ref.shape[1]
    gates = ((ga0, ga1), (gb0, gb1), (gm0, gm1))

    def conv_taps():
        u_scr[0:CARRY_ROWS, :] = u_scr[t:t + CARRY_ROWS, :]
        u_scr[CARRY_ROWS:CARRY_ROWS + t, :] = ac_ref[...].astype(F32) * ax_ref[...].astype(F32)

    def conv_rows(r0):
        rs = slice(r0, r0 + CONV_ROWS)
        y = u_scr[CARRY_ROWS - 2 + r0:CARRY_ROWS - 2 + r0 + CONV_ROWS, :] * cw_ref[0:1, :]
        y = y + u_scr[CARRY_ROWS - 1 + r0:CARRY_ROWS - 1 + r0 + CONV_ROWS, :] * cw_ref[1:2, :]
        y = y + u_scr[CARRY_ROWS + r0:CARRY_ROWS + r0 + CONV_ROWS, :] * cw_ref[2:3, :]
        ya = (ab_ref[rs, :].astype(F32) * y) * _silu(az_ref[rs, :].astype(F32))
        y_scr[rs, 0:W_A] = ya.astype(BF16)

    def band_norm(h):
        hs = slice(h * HEAD_DIM, (h + 1) * HEAD_DIM)
        for r in range(0, HALO, t):
            kw_scr[h, r:r + t, :] = kw_scr[h, r + t:r + 2 * t, :]
            vw_scr[h, r:r + t, :] = vw_scr[h, r + t:r + 2 * t, :]
        kw_scr[h, HALO:, 0:HEAD_DIM] = _rms(k_ref[:, hs].astype(F32), kg_ref[...]).astype(BF16)
        kw_scr[h, HALO:, HEAD_DIM:] = jnp.zeros((t, HEAD_DIM), BF16)
        vw_scr[h, HALO:, 0:HEAD_DIM] = v_ref[:, hs]
        vw_scr[h, HALO:, HEAD_DIM:] = jnp.ones((t, HEAD_DIM), BF16)
        qn_scr[h, :, 0:HEAD_DIM] = _rms(q_ref[:, hs].astype(F32), qg_ref[...] * scale).astype(BF16)

    def band_scores(h):
        ps = []
        for g in range(STEP_GROUPS):
            r0 = g * GROUP_ROWS
            s = lax.dot_general(qn_scr[h, r0:r0 + GROUP_ROWS, :], kw_scr[h, r0:r0 + WINDOW, :],
                                (((1,), (1,)), ((), ())), preferred_element_type=F32)
            s = s + bias_ref[h]
            ps.append(jnp.exp2(s - jnp.max(s, axis=-1, keepdims=True)).astype(BF16))
        return ps

    def band_values(h, ps):
        hs = slice(h * HEAD_DIM, (h + 1) * HEAD_DIM)
        for g, p in enumerate(ps):
            r0 = g * GROUP_ROWS
            rs = slice(r0, r0 + GROUP_ROWS)
            od = jnp.dot(p, vw_scr[h, r0:r0 + WINDOW, :], preferred_element_type=F32)
            o = od[:, 0:HEAD_DIM] / od[:, HEAD_DIM:]
            y_scr[rs, W_A + h * HEAD_DIM:W_A + (h + 1) * HEAD_DIM] = (
                o * _silu(bz_ref[rs, hs].astype(F32))).astype(BF16)

    def mem_norm(h):
        hs = slice(h * HEAD_DIM, (h + 1) * HEAD_DIM)
        mqn_scr[h] = _rms(mq_ref[:, hs].astype(F32), mqg_ref[...] * scale).astype(BF16)

    def mem_scores(h):
        mk = mk_ref[0, 0, :, h * HEAD_DIM:(h + 1) * HEAD_DIM]
        ps = []
        for r0 in range(0, t, MEM_ROWS):
            s = lax.dot_general(mqn_scr[h, r0:r0 + MEM_ROWS, :], mk, (((1,), (1,)), ((), ())),
                                preferred_element_type=F32)
            ps.append(jnp.exp2(s - jnp.max(s, axis=-1, keepdims=True)).astype(BF16))
        return ps

    def mem_values(h, ps):
        hs = slice(h * HEAD_DIM, (h + 1) * HEAD_DIM)
        mv = mv_ref[0, 0, h]
        for i, p in enumerate(ps):
            rs = slice(i * MEM_ROWS, (i + 1) * MEM_ROWS)
            od = jnp.dot(p, mv, preferred_element_type=F32)
            o = od[:, 0:HEAD_DIM] / od[:, HEAD_DIM:]
            c = W_A + W_B + h * HEAD_DIM
            y_scr[rs, c:c + HEAD_DIM] = (o * _silu(mz_ref[rs, hs].astype(F32))).astype(BF16)

    def merge(j):
        c = j * MERGE_COLS
        cs = slice(c, c + MERGE_COLS)
        gs = slice(c % GATE_BLOCK, c % GATE_BLOCK + MERGE_COLS)
        blk = c // GATE_BLOCK
        pa = jnp.dot(yp_scr[:, 0:W_A], wa_ref[:, cs], preferred_element_type=F32)
        acc = _sigmoid(gates[0][blk][:, gs].astype(F32) + bg_ref[0:1, cs]) * pa
        pb = jnp.dot(yp_scr[:, W_A:W_A + W_B], wb_ref[:, cs], preferred_element_type=F32)
        acc = acc + _sigmoid(gates[1][blk][:, gs].astype(F32) + bg_ref[1:2, cs]) * pb
        pm = jnp.dot(yp_scr[:, W_A + W_B:], wm_ref[:, cs], preferred_element_type=F32)
        acc = acc + _sigmoid(gates[2][blk][:, gs].astype(F32) + bg_ref[2:3, cs]) * pm
        m_scr[:, cs] = acc.astype(BF16)

    def out(j):
        cs = slice(j * OUT_COLS, (j + 1) * OUT_COLS)
        o_ref[:, cs] = x_ref[:, cs] + jnp.dot(m_scr[...], wo_ref[:, cs], preferred_element_type=F32)

    heads = ([(band_norm, band_scores, band_values, h) for h in range(n_heads_b)]
             + [(mem_norm, mem_scores, mem_values, h) for h in range(n_heads_m)])
    fillers = ([functools.partial(merge, j) for j in range(d // MERGE_COLS)]
               + [functools.partial(out, j) for j in range(d // OUT_COLS)])
    assert len(fillers) == len(heads)
    conv_units = [conv_taps] + [functools.partial(conv_rows, r0) for r0 in range(0, t, CONV_ROWS)]
    pending = list(fillers)
    pending.pop(0)()
    for norm, _, _, h in heads[:NORM_AHEAD]:
        norm(h)
    ps_next = heads[0][1](heads[0][3])
    for i, (_, _, values, h) in enumerate(heads):
        ps = ps_next
        if i + NORM_AHEAD < len(heads):
            norm_ahead, _, _, h_ahead = heads[i + NORM_AHEAD]
            norm_ahead(h_ahead)
        if i != MERGE_DRAIN_SLOT and pending:
            pending.pop(0)()
        if i >= CONV_FIRST_SLOT and conv_units:
            conv_units.pop(0)()
        if i + 1 < len(heads):
            _, scores_next, _, h_next = heads[i + 1]
            ps_next = scores_next(h_next)
        values(h, ps)
    assert not conv_units and not pending


def _mix_merge_call(proj, xf, mk, mv, bias, conv_w, q_g, k_g, mq_g, b_gate, w_a, w_b, w_m, w_o,
                    layer, *, seq, cols):
    n_tok, d = xf.shape
    t = STEP
    n_tiles = n_tok // t
    tiles_per_seq = seq // t
    n_heads = W_B // HEAD_DIM
    n_heads_m = W_M // HEAD_DIM
    mem_len = mk.shape[2]

    def mix_tile(s):
        return jnp.minimum(s, n_tiles - 1)

    def merge_tile(s):
        return jnp.maximum(s - 1, 0)

    def mix_spec(width, col):
        assert col % width == 0
        return pl.BlockSpec((t, width), lambda s: (mix_tile(s), col // width))

    assert cols["gate"] == 0
    resident = dict(pipeline_mode=pl.Buffered(1))
    in_specs = [
        mix_spec(4 * W_A, cols["conv"]), mix_spec(4 * W_B, cols["attn"]), mix_spec(2 * W_M, cols["mem"]),
        pl.BlockSpec((1, 1, mem_len, W_M), lambda s: (layer, mix_tile(s) // tiles_per_seq, 0, 0)),
        pl.BlockSpec((1, 1, n_heads_m, mem_len, 2 * HEAD_DIM),
                     lambda s: (layer, mix_tile(s) // tiles_per_seq, 0, 0, 0)),
        pl.BlockSpec((None, n_heads, GROUP_ROWS, WINDOW), lambda s: (layer, 0, 0, 0), **resident),
        pl.BlockSpec((None, CONV_WIDTH, W_A), lambda s: (layer, 0, 0)),
        pl.BlockSpec((None, 1, HEAD_DIM), lambda s: (layer, 0, 0)),
        pl.BlockSpec((None, 1, HEAD_DIM), lambda s: (layer, 0, 0)),
        pl.BlockSpec((None, 1, HEAD_DIM), lambda s: (layer, 0, 0)),
        pl.BlockSpec((t, 3 * d), lambda s: (merge_tile(s), 0)),
        pl.BlockSpec((t, d), lambda s: (merge_tile(s), 0)),
        pl.BlockSpec((None, 3, d), lambda s: (layer, 0, 0)),
        pl.BlockSpec((None, W_A, d), lambda s: (layer, 0, 0), **resident),
        pl.BlockSpec((None, W_B, d), lambda s: (layer, 0, 0), **resident),
        pl.BlockSpec((None, W_M, d), lambda s: (layer, 0, 0), **resident),
        pl.BlockSpec((None, d, d), lambda s: (layer, 0, 0), **resident),
    ]
    return pl.pallas_call(
        functools.partial(_mix_merge_kernel, n_tiles=n_tiles, tiles_per_seq=tiles_per_seq),
        grid=(n_tiles + 1,),
        in_specs=in_specs,
        out_specs=pl.BlockSpec((t, d), lambda s: (merge_tile(s), 0)),
        out_shape=jax.ShapeDtypeStruct((n_tok, d), F32),
        scratch_shapes=[
            pltpu.VMEM((t + CARRY_ROWS, W_A), F32),
            pltpu.VMEM((n_heads, HALO + t, KEY_LANES), BF16),
            pltpu.VMEM((n_heads, HALO + t, 2 * HEAD_DIM), BF16),
            pltpu.VMEM((n_heads, t, KEY_LANES), BF16),
            pltpu.VMEM((n_heads_m, t, HEAD_DIM), BF16),
            pltpu.VMEM((t, W_A + W_B + W_M), BF16),
            pltpu.VMEM((t, W_A + W_B + W_M), BF16),
            pltpu.VMEM((t, d), BF16),
        ],
        compiler_params=_params(1),
        name="mix_merge",
    )(proj, proj, proj, mk, mv, bias, conv_w, q_g, k_g, mq_g, proj, xf, b_gate, w_a, w_b, w_m, w_o)


def kernel(x, mem, norm_g, w_in, b_gate, conv_w, q_norm_g, k_norm_g, rel_table, mem_norm_g,
           w_mem_kv, mq_norm_g, mk_norm_g, w_branch_a, w_branch_b, w_branch_m, w_out):
    bsz, seq, d = x.shape
    n_layers = w_in.shape[0]
    n_tok = bsz * seq
    assert seq % HALO == 0 and HALO % STEP == 0 and d == 2 * GATE_BLOCK

    n_mix = 4 * W_A + 4 * W_B + 2 * W_M
    n_gate = 3 * d
    cols = {"gate": 0, "conv": n_gate, "attn": n_gate + 4 * W_A, "mem": n_gate + 4 * W_A + 4 * W_B}
    w_in_b = w_in.astype(BF16)
    w_kv = w_mem_kv.astype(BF16)
    w_a, w_b, w_m, w_o = (w.astype(BF16) for w in (w_branch_a, w_branch_b, w_branch_m, w_out))
    norm_g3 = norm_g.reshape(n_layers, 1, d)
    q_g = q_norm_g.reshape(n_layers, 1, HEAD_DIM)
    k_g = k_norm_g.reshape(n_layers, 1, HEAD_DIM)
    mq_g = mq_norm_g.reshape(n_layers, 1, HEAD_DIM)

    tm_in = min(2048, n_tok)
    tn_in = 1024

    bias = _bias_call(rel_table)
    mk, mv = _memkv_call(mem, mem_norm_g, w_kv, mk_norm_g)

    xf = x.reshape(n_tok, d)
    for layer in range(n_layers):
        proj = _inproj_call(xf, norm_g3, w_in_b, layer, tm=tm_in, tn=tn_in, col_shift=n_gate // tn_in)
        xf = _mix_merge_call(proj, xf, mk, mv, bias, conv_w, q_g, k_g, mq_g, b_gate,
                             w_a, w_b, w_m, w_o, layer, seq=seq, cols=cols)
    return xf.reshape(bsz, seq, d)
```

```python
import functools

import jax
import jax.numpy as jnp
from jax import lax
from jax.experimental import pallas as pl
from jax.experimental.pallas import tpu as pltpu

F32 = jnp.float32
BF16 = jnp.bfloat16

CHUNK = 64
N_PREV = 8
N_BAND = N_PREV + 1
HEAD_DIM = 128
W_A = 512
W_B = 1024
W_M = 512
CONV_WIDTH = 3
MAX_REL = 256
EPS = 1e-6
LOG2E = 1.4426950408889634
MASK_VALUE = -1e30

GROUP_CHUNKS = 2
GROUP_ROWS = GROUP_CHUNKS * CHUNK
WINDOW = (N_BAND + GROUP_CHUNKS - 1) * CHUNK
HALO = N_PREV * CHUNK
BIAS_LANES = 1024
LANES = 128
BIAS_CONST_LANES = (N_PREV * CHUNK + CHUNK - MAX_REL) // LANES * LANES

VMEM_LIMIT_BYTES = 58 * 1024 * 1024


def _params(n_axes):
    return pltpu.CompilerParams(
        dimension_semantics=("arbitrary",) * n_axes,
        vmem_limit_bytes=VMEM_LIMIT_BYTES,
    )


def _sigmoid(x):
    return 1.0 / (1.0 + jnp.exp2(x * -LOG2E))


def _silu(x):
    return x * _sigmoid(x)


def _rms(x, g):
    ms = jnp.mean(x * x, axis=-1, keepdims=True)
    return (x * lax.rsqrt(ms + EPS)) * g


def _bias_kernel(table_ref, o_ref, *, n_heads):
    l = pl.program_id(0)
    n_rel = 2 * MAX_REL + 1
    base = l * (n_rel * n_heads)
    lane = BIAS_CONST_LANES + lax.broadcasted_iota(jnp.int32, (8, WINDOW - BIAS_CONST_LANES), 1)
    rel = N_PREV * CHUNK - (lane - CHUNK)
    idx = jnp.clip(rel, -MAX_REL, MAX_REL) + MAX_REL
    lo = MAX_REL - (CHUNK - 1)

    def fill(r, accs):
        hit = idx == r
        return tuple(jnp.where(hit, table_ref[base + r * n_heads + h], acc) for h, acc in enumerate(accs))

    zeros = jnp.zeros((8, WINDOW - BIAS_CONST_LANES), F32)
    varying = lax.fori_loop(lo, n_rel, fill, (zeros,) * n_heads)
    col = lax.broadcasted_iota(jnp.int32, (CHUNK, WINDOW), 1)
    for h in range(n_heads):
        far = jnp.full((8, BIAS_CONST_LANES), table_ref[base + (n_rel - 1) * n_heads + h], F32)
        f_row = jnp.concatenate([far, varying[h], jnp.zeros((8, BIAS_LANES - WINDOW), F32)], axis=1)
        f_rows = jnp.broadcast_to(f_row[0:1, :] * LOG2E, (CHUNK, BIAS_LANES))
        r0 = pltpu.roll(f_rows, BIAS_LANES - CHUNK, 1, stride=1, stride_axis=0)[:, :WINDOW]
        o_ref[h, 0:CHUNK, :] = jnp.where(col < N_BAND * CHUNK, r0, MASK_VALUE)
        r1 = pltpu.roll(f_rows, 0, 1, stride=1, stride_axis=0)[:, :WINDOW]
        o_ref[h, CHUNK:GROUP_ROWS, :] = jnp.where(col >= CHUNK, r1, MASK_VALUE)


def _bias_call(rel_table):
    n_layers, n_rel, n_heads = rel_table.shape
    return pl.pallas_call(
        functools.partial(_bias_kernel, n_heads=n_heads),
        grid=(n_layers,),
        in_specs=[pl.BlockSpec(memory_space=pltpu.SMEM)],
        out_specs=pl.BlockSpec((None, n_heads, GROUP_ROWS, WINDOW), lambda l: (l, 0, 0, 0)),
        out_shape=jax.ShapeDtypeStruct((n_layers, n_heads, GROUP_ROWS, WINDOW), F32),
        compiler_params=_params(1),
        name="rel_bias",
    )(rel_table.reshape(-1))


def _memkv_kernel(mem_ref, g_ref, w_ref, kg_ref, mk_ref, mv_ref):
    h = _rms(mem_ref[0], g_ref[0]).astype(BF16)
    kv = jnp.dot(h, w_ref[0], preferred_element_type=F32)
    for hd in range(W_M // HEAD_DIM):
        cs = slice(hd * HEAD_DIM, (hd + 1) * HEAD_DIM)
        mk_ref[0, 0, :, cs] = _rms(kv[:, cs], kg_ref[0]).astype(BF16)
        mv_ref[0, 0, hd, :, 0:HEAD_DIM] = kv[:, W_M + hd * HEAD_DIM:W_M + (hd + 1) * HEAD_DIM].astype(BF16)
        mv_ref[0, 0, hd, :, HEAD_DIM:] = jnp.ones((kv.shape[0], HEAD_DIM), BF16)


def _memkv_call(mem, mem_norm_g, w_mem_kv, mk_norm_g):
    bsz, mem_len, d = mem.shape
    n_layers = w_mem_kv.shape[0]
    n_heads = W_M // HEAD_DIM
    return pl.pallas_call(
        _memkv_kernel,
        grid=(n_layers, bsz),
        in_specs=[
            pl.BlockSpec((1, mem_len, d), lambda l, b: (b, 0, 0)),
            pl.BlockSpec((1, 1, d), lambda l, b: (l, 0, 0)),
            pl.BlockSpec((1, d, 2 * W_M), lambda l, b: (l, 0, 0)),
            pl.BlockSpec((1, 1, HEAD_DIM), lambda l, b: (l, 0, 0)),
        ],
        out_specs=[
            pl.BlockSpec((1, 1, mem_len, W_M), lambda l, b: (l, b, 0, 0)),
            pl.BlockSpec((1, 1, n_heads, mem_len, 2 * HEAD_DIM), lambda l, b: (l, b, 0, 0, 0)),
        ],
        out_shape=[
            jax.ShapeDtypeStruct((n_layers, bsz, mem_len, W_M), BF16),
            jax.ShapeDtypeStruct((n_layers, bsz, n_heads, mem_len, 2 * HEAD_DIM), BF16),
        ],
        compiler_params=_params(2),
        name="mem_kv",
    )(mem, mem_norm_g.reshape(n_layers, 1, d), w_mem_kv, mk_norm_g.reshape(n_layers, 1, HEAD_DIM))


NORM_ROWS = 128
NORM_CHUNK = 208
NORM_SUB = 16
DOT_ROWS = 1024
FIRST_HALF_STEP = 2
SECOND_HALF_STEP = 7


def _inproj_kernel(x_hbm, g_ref, w_ref, o_ref, xbuf, h_even, h_odd, sem, *, n_row_tiles, n_col_tiles):
    i = pl.program_id(0)
    j = pl.program_id(1)
    tm = xbuf.shape[0]
    half = tm // 2
    has_next = i + 1 < n_row_tiles
    assert (SECOND_HALF_STEP - FIRST_HALF_STEP) * NORM_CHUNK >= half
    assert (n_col_tiles - SECOND_HALF_STEP) * NORM_CHUNK >= half
    assert NORM_CHUNK % NORM_SUB == 0 and half % NORM_SUB == 0 and NORM_CHUNK <= half

    def x_copy(tile, part):
        return pltpu.make_async_copy(
            x_hbm.at[pl.ds(tile * tm + part * half, half), :],
            xbuf.at[pl.ds(part * half, half), :],
            sem.at[part])

    @pl.when((i == 0) & (j == 0))
    def _():
        x_copy(0, 0).start()
        x_copy(0, 1).start()
        x_copy(0, 0).wait()
        x_copy(0, 1).wait()

        def body(c, carry):
            r = pl.multiple_of(c * NORM_ROWS, NORM_ROWS)
            h_even[pl.ds(r, NORM_ROWS), :] = _rms(xbuf[pl.ds(r, NORM_ROWS), :], g_ref[0]).astype(BF16)
            return carry

        lax.fori_loop(0, tm // NORM_ROWS, body, 0)

    @pl.when(has_next & (j == 0))
    def _():
        x_copy(i + 1, 0).start()

    @pl.when(has_next & (j == FIRST_HALF_STEP))
    def _():
        x_copy(i + 1, 0).wait()

    @pl.when(has_next & (j == SECOND_HALF_STEP - 1))
    def _():
        x_copy(i + 1, 1).start()

    @pl.when(has_next & (j == SECOND_HALF_STEP))
    def _():
        x_copy(i + 1, 1).wait()

    last = half - NORM_CHUNK
    start = jnp.where(
        j < FIRST_HALF_STEP, half + last,
        jnp.where(j < SECOND_HALF_STEP,
                  jnp.minimum((j - FIRST_HALF_STEP) * NORM_CHUNK, last),
                  half + jnp.minimum((j - SECOND_HALF_STEP) * NORM_CHUNK, last)))
    start = pl.multiple_of(start, NORM_SUB)

    def step(h_cur, h_next):
        for k in range(0, NORM_CHUNK, NORM_SUB):
            rows = pl.ds(start + k, NORM_SUB)
            h_next[rows, :] = _rms(xbuf[rows, :], g_ref[0]).astype(BF16)
        dot_rows = min(DOT_ROWS, tm)
        for r in range(0, tm, dot_rows):
            o_ref[r:r + dot_rows, :] = jnp.dot(h_cur[r:r + dot_rows, :], w_ref[0],
                                               preferred_element_type=F32).astype(o_ref.dtype)

    @pl.when(i % 2 == 0)
    def _():
        step(h_even, h_odd)

    @pl.when(i % 2 == 1)
    def _():
        step(h_odd, h_even)


def _inproj_call(xf, norm_g, w_in, layer, *, tm, tn, col_shift):
    m, d = xf.shape
    n = w_in.shape[-1]
    n_col = n // tn
    return pl.pallas_call(
        functools.partial(_inproj_kernel, n_row_tiles=m // tm, n_col_tiles=n // tn),
        grid=(m // tm, n // tn),
        in_specs=[
            pl.BlockSpec(memory_space=pl.ANY),
            pl.BlockSpec((1, 1, d), lambda i, j: (layer, 0, 0)),
            pl.BlockSpec((1, d, tn), lambda i, j: (layer, 0, j)),
        ],
        out_specs=pl.BlockSpec((tm, tn), lambda i, j: (i, (j + col_shift) % n_col)),
        out_shape=jax.ShapeDtypeStruct((m, n), BF16),
        scratch_shapes=[
            pltpu.VMEM((tm, d), F32),
            pltpu.VMEM((tm, d), BF16),
            pltpu.VMEM((tm, d), BF16),
            pltpu.SemaphoreType.DMA((2,)),
        ],
        compiler_params=_params(2),
        name="in_proj",
    )(xf, norm_g, w_in)


STEP = 256
STEP_GROUPS = STEP // GROUP_ROWS
CARRY_ROWS = 8
MEM_ROWS = 128
MERGE_COLS = 512
GATE_BLOCK = 1024
KEY_LANES = 2 * HEAD_DIM
OUT_COLS = 256
NORM_AHEAD = 2
MERGE_DRAIN_SLOT = 3
CONV_ROWS = 128
CONV_FIRST_SLOT = 8

def _mix_merge_kernel(conv_ref, attn_ref, memq_ref, mk_ref, mv_ref, bias_ref, cw_ref, qg_ref, kg_ref, mqg_ref,
                      gate_ref, x_ref, bg_ref, wa_ref, wb_ref, wm_ref, wo_ref,
                      o_ref, u_scr, kw_scr, vw_scr, qn_scr, mqn_scr, y_scr, yp_scr, m_scr, *, n_tiles, tiles_per_seq):
    ab_ref, ac_ref, ax_ref, az_ref = (conv_ref.at[:, j * W_A:(j + 1) * W_A] for j in range(4))
    q_ref, k_ref, v_ref, bz_ref = (attn_ref.at[:, j * W_B:(j + 1) * W_B] for j in range(4))
    mq_ref, mz_ref = (memq_ref.at[:, j * W_M:(j + 1) * W_M] for j in range(2))
    ga0, ga1, gb0, gb1, gm0, gm1 = (gate_ref.at[:, j * GATE_BLOCK:(j + 1) * GATE_BLOCK] for j in range(6))
    t = STEP
    n_heads_b = W_B // HEAD_DIM
    n_heads_m = W_M // HEAD_DIM
    step = pl.program_id(0)
    tile = jnp.minimum(step, n_tiles - 1)
    seq_start = tile % tiles_per_seq == 0
    scale = HEAD_DIM ** -0.5 * LOG2E

    @pl.when(step == 0)
    def _():
        one_hot = jnp.where(lax.broadcasted_iota(jnp.int32, (t, HEAD_DIM), 1) == 0, 1.0, 0.0).astype(BF16)
        y_scr[...] = jnp.zeros(y_scr.shape, BF16)
        u_scr[...] = jnp.zeros(u_scr.shape, F32)
        for h in range(n_heads_b):
            kw_scr[h] = jnp.zeros(kw_scr.shape[1:], BF16)
            vw_scr[h] = jnp.zeros(vw_scr.shape[1:], BF16)
            qn_scr[h, :, HEAD_DIM:] = one_hot

    yp_scr[...] = y_scr[...]

    @pl.when(seq_start)
    def _():
        lane = lax.broadcasted_iota(jnp.int32, (HALO, KEY_LANES), 1)
        pen = jnp.where(lane == HEAD_DIM, MASK_VALUE, 0.0).astype(BF16)
        u_scr[t:t + CARRY_ROWS, :] = jnp.zeros((CARRY_ROWS, W_A), F32)
        for h in range(n_heads_b):
            kw_scr[h, t:, :] = pen
            vw_scr[h, t:, :] = jnp.zeros((HALO, KEY_LANES), BF16)

    d = o_ref.shape[1]
    gates = ((ga0, ga1), (gb0, gb1), (gm0, gm1))

    def conv_taps():
        u_scr[0:CARRY_ROWS, :] = u_scr[t:t + CARRY_ROWS, :]
        u_scr[CARRY_ROWS:CARRY_ROWS + t, :] = ac_ref[...].astype(F32) * ax_ref[...].astype(F32)

    def conv_rows(r0):
        rs = slice(r0, r0 + CONV_ROWS)
        y = u_scr[CARRY_ROWS - 2 + r0:CARRY_ROWS - 2 + r0 + CONV_ROWS, :] * cw_ref[0:1, :]
        y = y + u_scr[CARRY_ROWS - 1 + r0:CARRY_ROWS - 1 + r0 + CONV_ROWS, :] * cw_ref[1:2, :]
        y = y + u_scr[CARRY_ROWS + r0:CARRY_ROWS + r0 + CONV_ROWS, :] * cw_ref[2:3, :]
        ya = (ab_ref[rs, :].astype(F32) * y) * _silu(az_ref[rs, :].astype(F32))
        y_scr[rs, 0:W_A] = ya.astype(BF16)

    def band_norm(h):
        hs = slice(h * HEAD_DIM, (h + 1) * HEAD_DIM)
        for r in range(0, HALO, t):
            kw_scr[h, r:r + t, :] = kw_scr[h, r + t:r + 2 * t, :]
            vw_scr[h, r:r + t, :] = vw_scr[h, r + t:r + 2 * t, :]
        kw_scr[h, HALO:, 0:HEAD_DIM] = _rms(k_ref[:, hs].astype(F32), kg_ref[...]).astype(BF16)
        kw_scr[h, HALO:, HEAD_DIM:] = jnp.zeros((t, HEAD_DIM), BF16)
        vw_scr[h, HALO:, 0:HEAD_DIM] = v_ref[:, hs]
        vw_scr[h, HALO:, HEAD_DIM:] = jnp.ones((t, HEAD_DIM), BF16)
        qn_scr[h, :, 0:HEAD_DIM] = _rms(q_ref[:, hs].astype(F32), qg_ref[...] * scale).astype(BF16)

    def band_scores(h):
        ps = []
        for g in range(STEP_GROUPS):
            r0 = g * GROUP_ROWS
            s = lax.dot_general(qn_scr[h, r0:r0 + GROUP_ROWS, :], kw_scr[h, r0:r0 + WINDOW, :],
                                (((1,), (1,)), ((), ())), preferred_element_type=F32)
            s = s + bias_ref[h]
            ps.append(jnp.exp2(s - jnp.max(s, axis=-1, keepdims=True)).astype(BF16))
        return ps

    def band_values(h, ps):
        hs = slice(h * HEAD_DIM, (h + 1) * HEAD_DIM)
        for g, p in enumerate(ps):
            r0 = g * GROUP_ROWS
            rs = slice(r0, r0 + GROUP_ROWS)
            od = jnp.dot(p, vw_scr[h, r0:r0 + WINDOW, :], preferred_element_type=F32)
            o = od[:, 0:HEAD_DIM] / od[:, HEAD_DIM:]
            y_scr[rs, W_A + h * HEAD_DIM:W_A + (h + 1) * HEAD_DIM] = (
                o * _silu(bz_ref[rs, hs].astype(F32))).astype(BF16)

    def mem_norm(h):
        hs = slice(h * HEAD_DIM, (h + 1) * HEAD_DIM)
        mqn_scr[h] = _rms(mq_ref[:, hs].astype(F32), mqg_ref[...] * scale).astype(BF16)

    def mem_scores(h):
        mk = mk_ref[0, 0, :, h * HEAD_DIM:(h + 1) * HEAD_DIM]
        ps = []
        for r0 in range(0, t, MEM_ROWS):
            s = lax.dot_general(mqn_scr[h, r0:r0 + MEM_ROWS, :], mk, (((1,), (1,)), ((), ())),
                                preferred_element_type=F32)
            ps.append(jnp.exp2(s - jnp.max(s, axis=-1, keepdims=True)).astype(BF16))
        return ps

    def mem_values(h, ps):
        hs = slice(h * HEAD_DIM, (h + 1) * HEAD_DIM)
        mv = mv_ref[0, 0, h]
        for i, p in enumerate(ps):
            rs = slice(i * MEM_ROWS, (i + 1) * MEM_ROWS)
            od = jnp.dot(p, mv, preferred_element_type=F32)
            o = od[:, 0:HEAD_DIM] / od[:, HEAD_DIM:]
            c = W_A + W_B + h * HEAD_DIM
            y_scr[rs, c:c + HEAD_DIM] = (o * _silu(mz_ref[rs, hs].astype(F32))).astype(BF16)

    def merge(j):
        c = j * MERGE_COLS
        cs = slice(c, c + MERGE_COLS)
        gs = slice(c % GATE_BLOCK, c % GATE_BLOCK + MERGE_COLS)
        blk = c // GATE_BLOCK
        pa = jnp.dot(yp_scr[:, 0:W_A], wa_ref[:, cs], preferred_element_type=F32)
        acc = _sigmoid(gates[0][blk][:, gs].astype(F32) + bg_ref[0:1, cs]) * pa
        pb = jnp.dot(yp_scr[:, W_A:W_A + W_B], wb_ref[:, cs], preferred_element_type=F32)
        acc = acc + _sigmoid(gates[1][blk][:, gs].astype(F32) + bg_ref[1:2, cs]) * pb
        pm = jnp.dot(yp_scr[:, W_A + W_B:], wm_ref[:, cs], preferred_element_type=F32)
        acc = acc + _sigmoid(gates[2][blk][:, gs].astype(F32) + bg_ref[2:3, cs]) * pm
        m_scr[:, cs] = acc.astype(BF16)

    def out(j):
        cs = slice(j * OUT_COLS, (j + 1) * OUT_COLS)
        o_ref[:, cs] = x_ref[:, cs] + jnp.dot(m_scr[...], wo_ref[:, cs], preferred_element_type=F32)

    heads = ([(band_norm, band_scores, band_values, h) for h in range(n_heads_b)]
             + [(mem_norm, mem_scores, mem_values, h) for h in range(n_heads_m)])
    fillers = ([functools.partial(merge, j) for j in range(d // MERGE_COLS)]
               + [functools.partial(out, j) for j in range(d // OUT_COLS)])
    assert len(fillers) == len(heads)
    conv_units = [conv_taps] + [functools.partial(conv_rows, r0) for r0 in range(0, t, CONV_ROWS)]
    pending = list(fillers)
    pending.pop(0)()
    for norm, _, _, h in heads[:NORM_AHEAD]:
        norm(h)
    ps_next = heads[0][1](heads[0][3])
    for i, (_, _, values, h) in enumerate(heads):
        ps = ps_next
        if i + NORM_AHEAD < len(heads):
            norm_ahead, _, _, h_ahead = heads[i + NORM_AHEAD]
            norm_ahead(h_ahead)
        if i != MERGE_DRAIN_SLOT and pending:
            pending.pop(0)()
        if i >= CONV_FIRST_SLOT and conv_units:
            conv_units.pop(0)()
        if i + 1 < len(heads):
            _, scores_next, _, h_next = heads[i + 1]
            ps_next = scores_next(h_next)
        values(h, ps)
    assert not conv_units and not pending


def _mix_merge_call(proj, xf, mk, mv, bias, conv_w, q_g, k_g, mq_g, b_gate, w_a, w_b, w_m, w_o,
                    layer, *, seq, cols):
    n_tok, d = xf.shape
    t = STEP
    n_tiles = n_tok // t
    tiles_per_seq = seq // t
    n_heads = W_B // HEAD_DIM
    n_heads_m = W_M // HEAD_DIM
    mem_len = mk.shape[2]

    def mix_tile(s):
        return jnp.minimum(s, n_tiles - 1)

    def merge_tile(s):
        return jnp.maximum(s - 1, 0)

    def mix_spec(width, col):
        assert col % width == 0
        return pl.BlockSpec((t, width), lambda s: (mix_tile(s), col // width))

    assert cols["gate"] == 0
    resident = dict(pipeline_mode=pl.Buffered(1))
    in_specs = [
        mix_spec(4 * W_A, cols["conv"]), mix_spec(4 * W_B, cols["attn"]), mix_spec(2 * W_M, cols["mem"]),
        pl.BlockSpec((1, 1, mem_len, W_M), lambda s: (layer, mix_tile(s) // tiles_per_seq, 0, 0)),
        pl.BlockSpec((1, 1, n_heads_m, mem_len, 2 * HEAD_DIM),
                     lambda s: (layer, mix_tile(s) // tiles_per_seq, 0, 0, 0)),
        pl.BlockSpec((None, n_heads, GROUP_ROWS, WINDOW), lambda s: (layer, 0, 0, 0), **resident),
        pl.BlockSpec((None, CONV_WIDTH, W_A), lambda s: (layer, 0, 0)),
        pl.BlockSpec((None, 1, HEAD_DIM), lambda s: (layer, 0, 0)),
        pl.BlockSpec((None, 1, HEAD_DIM), lambda s: (layer, 0, 0)),
        pl.BlockSpec((None, 1, HEAD_DIM), lambda s: (layer, 0, 0)),
        pl.BlockSpec((t, 3 * d), lambda s: (merge_tile(s), 0)),
        pl.BlockSpec((t, d), lambda s: (merge_tile(s), 0)),
        pl.BlockSpec((None, 3, d), lambda s: (layer, 0, 0)),
        pl.BlockSpec((None, W_A, d), lambda s: (layer, 0, 0), **resident),
        pl.BlockSpec((None, W_B, d), lambda s: (layer, 0, 0), **resident),
        pl.BlockSpec((None, W_M, d), lambda s: (layer, 0, 0), **resident),
        pl.BlockSpec((None, d, d), lambda s: (layer, 0, 0), **resident),
    ]
    return pl.pallas_call(
        functools.partial(_mix_merge_kernel, n_tiles=n_tiles, tiles_per_seq=tiles_per_seq),
        grid=(n_tiles + 1,),
        in_specs=in_specs,
        out_specs=pl.BlockSpec((t, d), lambda s: (merge_tile(s), 0)),
        out_shape=jax.ShapeDtypeStruct((n_tok, d), F32),
        scratch_shapes=[
            pltpu.VMEM((t + CARRY_ROWS, W_A), F32),
            pltpu.VMEM((n_heads, HALO + t, KEY_LANES), BF16),
            pltpu.VMEM((n_heads, HALO + t, 2 * HEAD_DIM), BF16),
            pltpu.VMEM((n_heads, t, KEY_LANES), BF16),
            pltpu.VMEM((n_heads_m, t, HEAD_DIM), BF16),
            pltpu.VMEM((t, W_A + W_B + W_M), BF16),
            pltpu.VMEM((t, W_A + W_B + W_M), BF16),
            pltpu.VMEM((t, d), BF16),
        ],
        compiler_params=_params(1),
        name="mix_merge",
    )(proj, proj, proj, mk, mv, bias, conv_w, q_g, k_g, mq_g, proj, xf, b_gate, w_a, w_b, w_m, w_o)


def kernel(x, mem, norm_g, w_in, b_gate, conv_w, q_norm_g, k_norm_g, rel_table, mem_norm_g,
           w_mem_kv, mq_norm_g, mk_norm_g, w_branch_a, w_branch_b, w_branch_m, w_out):
    bsz, seq, d = x.shape
    n_layers = w_in.shape[0]
    n_tok = bsz * seq
    assert seq % HALO == 0 and HALO % STEP == 0 and d == 2 * GATE_BLOCK

    n_mix = 4 * W_A + 4 * W_B + 2 * W_M
    n_gate = 3 * d
    cols = {"gate": 0, "conv": n_gate, "attn": n_gate + 4 * W_A, "mem": n_gate + 4 * W_A + 4 * W_B}
    w_in_b = w_in.astype(BF16)
    w_kv = w_mem_kv.astype(BF16)
    w_a, w_b, w_m, w_o = (w.astype(BF16) for w in (w_branch_a, w_branch_b, w_branch_m, w_out))
    norm_g3 = norm_g.reshape(n_layers, 1, d)
    q_g = q_norm_g.reshape(n_layers, 1, HEAD_DIM)
    k_g = k_norm_g.reshape(n_layers, 1, HEAD_DIM)
    mq_g = mq_norm_g.reshape(n_layers, 1, HEAD_DIM)

    tm_in = min(2048, n_tok)
    tn_in = 1024

    bias = _bias_call(rel_table)
    mk, mv = _memkv_call(mem, mem_norm_g, w_kv, mk_norm_g)

    xf = x.reshape(n_tok, d)
    for layer in range(n_layers):
        proj = _inproj_call(xf, norm_g3, w_in_b, layer, tm=tm_in, tn=tn_in, col_shift=n_gate // tn_in)
        xf = _mix_merge_call(proj, xf, mk, mv, bias, conv_w, q_g, k_g, mq_g, b_gate,
                             w_a, w_b, w_m, w_o, layer, seq=seq, cols=cols)
    return xf.reshape(bsz, seq, d)
```

```python
import functools

import jax
import jax.numpy as jnp
from jax import lax
from jax.experimental import pallas as pl
from jax.experimental.pallas import tpu as pltpu

F32 = jnp.float32
BF16 = jnp.bfloat16

CHUNK = 64
N_PREV = 8
N_BAND = N_PREV + 1
HEAD_DIM = 128
W_A = 512
W_B = 1024
W_M = 512
CONV_WIDTH = 3
MAX_REL = 256
EPS = 1e-6
LOG2E = 1.4426950408889634
MASK_VALUE = -1e30

GROUP_CHUNKS = 2
GROUP_ROWS = GROUP_CHUNKS * CHUNK
WINDOW = (N_BAND + GROUP_CHUNKS - 1) * CHUNK
HALO = N_PREV * CHUNK
BIAS_LANES = 1024
LANES = 128
BIAS_CONST_LANES = (N_PREV * CHUNK + CHUNK - MAX_REL) // LANES * LANES

VMEM_LIMIT_BYTES = 58 * 1024 * 1024


def _params(n_axes):
    return pltpu.CompilerParams(
        dimension_semantics=("arbitrary",) * n_axes,
        vmem_limit_bytes=VMEM_LIMIT_BYTES,
    )


def _sigmoid(x):
    return 1.0 / (1.0 + jnp.exp2(x * -LOG2E))


def _silu(x):
    return x * _sigmoid(x)


def _rms(x, g):
    ms = jnp.mean(x * x, axis=-1, keepdims=True)
    return (x * lax.rsqrt(ms + EPS)) * g


def _bias_kernel(table_ref, o_ref, *, n_heads):
    l = pl.program_id(0)
    n_rel = 2 * MAX_REL + 1
    base = l * (n_rel * n_heads)
    lane = BIAS_CONST_LANES + lax.broadcasted_iota(jnp.int32, (8, WINDOW - BIAS_CONST_LANES), 1)
    rel = N_PREV * CHUNK - (lane - CHUNK)
    idx = jnp.clip(rel, -MAX_REL, MAX_REL) + MAX_REL
    lo = MAX_REL - (CHUNK - 1)

    def fill(r, accs):
        hit = idx == r
        return tuple(jnp.where(hit, table_ref[base + r * n_heads + h], acc) for h, acc in enumerate(accs))

    zeros = jnp.zeros((8, WINDOW - BIAS_CONST_LANES), F32)
    varying = lax.fori_loop(lo, n_rel, fill, (zeros,) * n_heads)
    col = lax.broadcasted_iota(jnp.int32, (CHUNK, WINDOW), 1)
    for h in range(n_heads):
        far = jnp.full((8, BIAS_CONST_LANES), table_ref[base + (n_rel - 1) * n_heads + h], F32)
        f_row = jnp.concatenate([far, varying[h], jnp.zeros((8, BIAS_LANES - WINDOW), F32)], axis=1)
        f_rows = jnp.broadcast_to(f_row[0:1, :] * LOG2E, (CHUNK, BIAS_LANES))
        r0 = pltpu.roll(f_rows, BIAS_LANES - CHUNK, 1, stride=1, stride_axis=0)[:, :WINDOW]
        o_ref[h, 0:CHUNK, :] = jnp.where(col < N_BAND * CHUNK, r0, MASK_VALUE)
        r1 = pltpu.roll(f_rows, 0, 1, stride=1, stride_axis=0)[:, :WINDOW]
        o_ref[h, CHUNK:GROUP_ROWS, :] = jnp.where(col >= CHUNK, r1, MASK_VALUE)


def _bias_call(rel_table):
    n_layers, n_rel, n_heads = rel_table.shape
    return pl.pallas_call(
        functools.partial(_bias_kernel, n_heads=n_heads),
        grid=(n_layers,),
        in_specs=[pl.BlockSpec(memory_space=pltpu.SMEM)],
        out_specs=pl.BlockSpec((None, n_heads, GROUP_ROWS, WINDOW), lambda l: (l, 0, 0, 0)),
        out_shape=jax.ShapeDtypeStruct((n_layers, n_heads, GROUP_ROWS, WINDOW), F32),
        compiler_params=_params(1),
        name="rel_bias",
    )(rel_table.reshape(-1))


def _memkv_kernel(mem_ref, g_ref, w_ref, kg_ref, mk_ref, mv_ref):
    h = _rms(mem_ref[0], g_ref[0]).astype(BF16)
    kv = jnp.dot(h, w_ref[0], preferred_element_type=F32)
    for hd in range(W_M // HEAD_DIM):
        cs = slice(hd * HEAD_DIM, (hd + 1) * HEAD_DIM)
        mk_ref[0, 0, :, cs] = _rms(kv[:, cs], kg_ref[0]).astype(BF16)
        mv_ref[0, 0, hd, :, 0:HEAD_DIM] = kv[:, W_M + hd * HEAD_DIM:W_M + (hd + 1) * HEAD_DIM].astype(BF16)
        mv_ref[0, 0, hd, :, HEAD_DIM:] = jnp.ones((kv.shape[0], HEAD_DIM), BF16)


def _memkv_call(mem, mem_norm_g, w_mem_kv, mk_norm_g):
    bsz, mem_len, d = mem.shape
    n_layers = w_mem_kv.shape[0]
    n_heads = W_M // HEAD_DIM
    return pl.pallas_call(
        _memkv_kernel,
        grid=(n_layers, bsz),
        in_specs=[
            pl.BlockSpec((1, mem_len, d), lambda l, b: (b, 0, 0)),
            pl.BlockSpec((1, 1, d), lambda l, b: (l, 0, 0)),
            pl.BlockSpec((1, d, 2 * W_M), lambda l, b: (l, 0, 0)),
            pl.BlockSpec((1, 1, HEAD_DIM), lambda l, b: (l, 0, 0)),
        ],
        out_specs=[
            pl.BlockSpec((1, 1, mem_len, W_M), lambda l, b: (l, b, 0, 0)),
            pl.BlockSpec((1, 1, n_heads, mem_len, 2 * HEAD_DIM), lambda l, b: (l, b, 0, 0, 0)),
        ],
        out_shape=[
            jax.ShapeDtypeStruct((n_layers, bsz, mem_len, W_M), BF16),
            jax.ShapeDtypeStruct((n_layers, bsz, n_heads, mem_len, 2 * HEAD_DIM), BF16),
        ],
        compiler_params=_params(2),
        name="mem_kv",
    )(mem, mem_norm_g.reshape(n_layers, 1, d), w_mem_kv, mk_norm_g.reshape(n_layers, 1, HEAD_DIM))


IN_ROW_TILE = 2048
IN_COL_TILE = 1024
NORM_ROWS = 128
NORM_CHUNK = 208
NORM_SUB = 16
DOT_ROWS = 1024
FIRST_HALF_STEP = 2
SECOND_HALF_STEP = 7


def _inproj_kernel(x_hbm, g_ref, w_ref, o_ref, xbuf, h_even, h_odd, sem, *, n_row_tiles, n_col_tiles):
    i = pl.program_id(0)
    j = pl.program_id(1)
    tm = xbuf.shape[0]
    half = tm // 2
    has_next = i + 1 < n_row_tiles
    assert (SECOND_HALF_STEP - FIRST_HALF_STEP) * NORM_CHUNK >= half
    assert (n_col_tiles - SECOND_HALF_STEP) * NORM_CHUNK >= half
    assert NORM_CHUNK % NORM_SUB == 0 and half % NORM_SUB == 0 and NORM_CHUNK <= half

    def x_copy(tile, part):
        return pltpu.make_async_copy(
            x_hbm.at[pl.ds(tile * tm + part * half, half), :],
            xbuf.at[pl.ds(part * half, half), :],
            sem.at[part])

    @pl.when((i == 0) & (j == 0))
    def _():
        x_copy(0, 0).start()
        x_copy(0, 1).start()
        x_copy(0, 0).wait()
        x_copy(0, 1).wait()

        def body(c, carry):
            r = pl.multiple_of(c * NORM_ROWS, NORM_ROWS)
            h_even[pl.ds(r, NORM_ROWS), :] = _rms(xbuf[pl.ds(r, NORM_ROWS), :], g_ref[0]).astype(BF16)
            return carry

        lax.fori_loop(0, tm // NORM_ROWS, body, 0)

    @pl.when(has_next & (j == 0))
    def _():
        x_copy(i + 1, 0).start()

    @pl.when(has_next & (j == FIRST_HALF_STEP))
    def _():
        x_copy(i + 1, 0).wait()

    @pl.when(has_next & (j == SECOND_HALF_STEP - 1))
    def _():
        x_copy(i + 1, 1).start()

    @pl.when(has_next & (j == SECOND_HALF_STEP))
    def _():
        x_copy(i + 1, 1).wait()

    last = half - NORM_CHUNK
    start = jnp.where(
        j < FIRST_HALF_STEP, half + last,
        jnp.where(j < SECOND_HALF_STEP,
                  jnp.minimum((j - FIRST_HALF_STEP) * NORM_CHUNK, last),
                  half + jnp.minimum((j - SECOND_HALF_STEP) * NORM_CHUNK, last)))
    start = pl.multiple_of(start, NORM_SUB)

    def step(h_cur, h_next):
        for k in range(0, NORM_CHUNK, NORM_SUB):
            rows = pl.ds(start + k, NORM_SUB)
            h_next[rows, :] = _rms(xbuf[rows, :], g_ref[0]).astype(BF16)
        dot_rows = min(DOT_ROWS, tm)
        for r in range(0, tm, dot_rows):
            o_ref[r:r + dot_rows, :] = jnp.dot(h_cur[r:r + dot_rows, :], w_ref[0],
                                               preferred_element_type=F32).astype(o_ref.dtype)

    @pl.when(i % 2 == 0)
    def _():
        step(h_even, h_odd)

    @pl.when(i % 2 == 1)
    def _():
        step(h_odd, h_even)


def _inproj_call(xf, norm_g, w_in, layer, *, tm, tn, col_shift):
    m, d = xf.shape
    n = w_in.shape[-1]
    n_col = n // tn
    return pl.pallas_call(
        functools.partial(_inproj_kernel, n_row_tiles=m // tm, n_col_tiles=n // tn),
        grid=(m // tm, n // tn),
        in_specs=[
            pl.BlockSpec(memory_space=pl.ANY),
            pl.BlockSpec((1, 1, d), lambda i, j: (layer, 0, 0)),
            pl.BlockSpec((1, d, tn), lambda i, j: (layer, 0, j)),
        ],
        out_specs=pl.BlockSpec((tm, tn), lambda i, j: (i, (j + col_shift) % n_col)),
        out_shape=jax.ShapeDtypeStruct((m, n), BF16),
        scratch_shapes=[
            pltpu.VMEM((tm, d), F32),
            pltpu.VMEM((tm, d), BF16),
            pltpu.VMEM((tm, d), BF16),
            pltpu.SemaphoreType.DMA((2,)),
        ],
        compiler_params=_params(2),
        name="in_proj",
    )(xf, norm_g, w_in)


STEP = 256
STEP_GROUPS = STEP // GROUP_ROWS
CARRY_ROWS = 8
MEM_ROWS = 128
MERGE_COLS = 512
GATE_BLOCK = 1024
KEY_LANES = 2 * HEAD_DIM
OUT_COLS = 256
NORM_AHEAD = 2
MERGE_DRAIN_SLOT = 3
CONV_ROWS = 128
CONV_FIRST_SLOT = 8

def _mix_merge_kernel(conv_ref, attn_ref, memq_ref, mk_ref, mv_ref, bias_ref, cw_ref, qg_ref, kg_ref, mqg_ref,
                      gate_ref, x_ref, bg_ref, wa_ref, wb_ref, wm_ref, wo_ref,
                      o_ref, u_scr, kw_scr, vw_scr, qn_scr, mqn_scr, y_scr, yp_scr, m_scr, *, n_tiles, tiles_per_seq):
    ab_ref, ac_ref, ax_ref, az_ref = (conv_ref.at[:, j * W_A:(j + 1) * W_A] for j in range(4))
    q_ref, k_ref, v_ref, bz_ref = (attn_ref.at[:, j * W_B:(j + 1) * W_B] for j in range(4))
    mq_ref, mz_ref = (memq_ref.at[:, j * W_M:(j + 1) * W_M] for j in range(2))
    ga0, ga1, gb0, gb1, gm0, gm1 = (gate_ref.at[:, j * GATE_BLOCK:(j + 1) * GATE_BLOCK] for j in range(6))
    t = STEP
    n_heads_b = W_B // HEAD_DIM
    n_heads_m = W_M // HEAD_DIM
    step = pl.program_id(0)
    tile = jnp.minimum(step, n_tiles - 1)
    seq_start = tile % tiles_per_seq == 0
    scale = HEAD_DIM ** -0.5 * LOG2E

    @pl.when(step == 0)
    def _():
        one_hot = jnp.where(lax.broadcasted_iota(jnp.int32, (t, HEAD_DIM), 1) == 0, 1.0, 0.0).astype(BF16)
        y_scr[...] = jnp.zeros(y_scr.shape, BF16)
        u_scr[...] = jnp.zeros(u_scr.shape, F32)
        for h in range(n_heads_b):
            kw_scr[h] = jnp.zeros(kw_scr.shape[1:], BF16)
            vw_scr[h] = jnp.zeros(vw_scr.shape[1:], BF16)
            qn_scr[h, :, HEAD_DIM:] = one_hot

    yp_scr[...] = y_scr[...]

    @pl.when(seq_start)
    def _():
        lane = lax.broadcasted_iota(jnp.int32, (HALO, KEY_LANES), 1)
        pen = jnp.where(lane == HEAD_DIM, MASK_VALUE, 0.0).astype(BF16)
        u_scr[t:t + CARRY_ROWS, :] = jnp.zeros((CARRY_ROWS, W_A), F32)
        for h in range(n_heads_b):
            kw_scr[h, t:, :] = pen
            vw_scr[h, t:, :] = jnp.zeros((HALO, KEY_LANES), BF16)

    d = o_ref.shape[1]
    gates = ((ga0, ga1), (gb0, gb1), (gm0, gm1))

    def conv_taps():
        u_scr[0:CARRY_ROWS, :] = u_scr[t:t + CARRY_ROWS, :]
        u_scr[CARRY_ROWS:CARRY_ROWS + t, :] = ac_ref[...].astype(F32) * ax_ref[...].astype(F32)

    def conv_rows(r0):
        rs = slice(r0, r0 + CONV_ROWS)
        y = u_scr[CARRY_ROWS - 2 + r0:CARRY_ROWS - 2 + r0 + CONV_ROWS, :] * cw_ref[0:1, :]
        y = y + u_scr[CARRY_ROWS - 1 + r0:CARRY_ROWS - 1 + r0 + CONV_ROWS, :] * cw_ref[1:2, :]
        y = y + u_scr[CARRY_ROWS + r0:CARRY_ROWS + r0 + CONV_ROWS, :] * cw_ref[2:3, :]
        ya = (ab_ref[rs, :].astype(F32) * y) * _silu(az_ref[rs, :].astype(F32))
        y_scr[rs, 0:W_A] = ya.astype(BF16)

    def band_norm(h):
        hs = slice(h * HEAD_DIM, (h + 1) * HEAD_DIM)
        for r in range(0, HALO, t):
            kw_scr[h, r:r + t, :] = kw_scr[h, r + t:r + 2 * t, :]
            vw_scr[h, r:r + t, :] = vw_scr[h, r + t:r + 2 * t, :]
        kw_scr[h, HALO:, 0:HEAD_DIM] = _rms(k_ref[:, hs].astype(F32), kg_ref[...]).astype(BF16)
        kw_scr[h, HALO:, HEAD_DIM:] = jnp.zeros((t, HEAD_DIM), BF16)
        vw_scr[h, HALO:, 0:HEAD_DIM] = v_ref[:, hs]
        vw_scr[h, HALO:, HEAD_DIM:] = jnp.ones((t, HEAD_DIM), BF16)
        qn_scr[h, :, 0:HEAD_DIM] = _rms(q_ref[:, hs].astype(F32), qg_ref[...] * scale).astype(BF16)

    def band_scores(h):
        ps = []
        for g in range(STEP_GROUPS):
            r0 = g * GROUP_ROWS
            s = lax.dot_general(qn_scr[h, r0:r0 + GROUP_ROWS, :], kw_scr[h, r0:r0 + WINDOW, :],
                                (((1,), (1,)), ((), ())), preferred_element_type=F32)
            s = s + bias_ref[h]
            ps.append(jnp.exp2(s - jnp.max(s, axis=-1, keepdims=True)).astype(BF16))
        return ps

    def band_values(h, ps):
        hs = slice(h * HEAD_DIM, (h + 1) * HEAD_DIM)
        for g, p in enumerate(ps):
            r0 = g * GROUP_ROWS
            rs = slice(r0, r0 + GROUP_ROWS)
            od = jnp.dot(p, vw_scr[h, r0:r0 + WINDOW, :], preferred_element_type=F32)
            o = od[:, 0:HEAD_DIM] / od[:, HEAD_DIM:]
            y_scr[rs, W_A + h * HEAD_DIM:W_A + (h + 1) * HEAD_DIM] = (
                o * _silu(bz_ref[rs, hs].astype(F32))).astype(BF16)

    def mem_norm(h):
        hs = slice(h * HEAD_DIM, (h + 1) * HEAD_DIM)
        mqn_scr[h] = _rms(mq_ref[:, hs].astype(F32), mqg_ref[...] * scale).astype(BF16)

    def mem_scores(h):
        mk = mk_ref[0, 0, :, h * HEAD_DIM:(h + 1) * HEAD_DIM]
        ps = []
        for r0 in range(0, t, MEM_ROWS):
            s = lax.dot_general(mqn_scr[h, r0:r0 + MEM_ROWS, :], mk, (((1,), (1,)), ((), ())),
                                preferred_element_type=F32)
            ps.append(jnp.exp2(s - jnp.max(s, axis=-1, keepdims=True)).astype(BF16))
        return ps

    def mem_values(h, ps):
        hs = slice(h * HEAD_DIM, (h + 1) * HEAD_DIM)
        mv = mv_ref[0, 0, h]
        for i, p in enumerate(ps):
            rs = slice(i * MEM_ROWS, (i + 1) * MEM_ROWS)
            od = jnp.dot(p, mv, preferred_element_type=F32)
            o = od[:, 0:HEAD_DIM] / od[:, HEAD_DIM:]
            c = W_A + W_B + h * HEAD_DIM
            y_scr[rs, c:c + HEAD_DIM] = (o * _silu(mz_ref[rs, hs].astype(F32))).astype(BF16)

    def merge(j):
        c = j * MERGE_COLS
        cs = slice(c, c + MERGE_COLS)
        gs = slice(c % GATE_BLOCK, c % GATE_BLOCK + MERGE_COLS)
        blk = c // GATE_BLOCK
        pa = jnp.dot(yp_scr[:, 0:W_A], wa_ref[:, cs], preferred_element_type=F32)
        acc = _sigmoid(gates[0][blk][:, gs].astype(F32) + bg_ref[0:1, cs]) * pa
        pb = jnp.dot(yp_scr[:, W_A:W_A + W_B], wb_ref[:, cs], preferred_element_type=F32)
        acc = acc + _sigmoid(gates[1][blk][:, gs].astype(F32) + bg_ref[1:2, cs]) * pb
        pm = jnp.dot(yp_scr[:, W_A + W_B:], wm_ref[:, cs], preferred_element_type=F32)
        acc = acc + _sigmoid(gates[2][blk][:, gs].astype(F32) + bg_ref[2:3, cs]) * pm
        m_scr[:, cs] = acc.astype(BF16)

    def out(j):
        cs = slice(j * OUT_COLS, (j + 1) * OUT_COLS)
        o_ref[:, cs] = x_ref[:, cs] + jnp.dot(m_scr[...], wo_ref[:, cs], preferred_element_type=F32)

    heads = ([(band_norm, band_scores, band_values, h) for h in range(n_heads_b)]
             + [(mem_norm, mem_scores, mem_values, h) for h in range(n_heads_m)])
    fillers = ([functools.partial(merge, j) for j in range(d // MERGE_COLS)]
               + [functools.partial(out, j) for j in range(d // OUT_COLS)])
    assert len(fillers) == len(heads)
    conv_units = [conv_taps] + [functools.partial(conv_rows, r0) for r0 in range(0, t, CONV_ROWS)]
    pending = list(fillers)
    pending.pop(0)()
    for norm, _, _, h in heads[:NORM_AHEAD]:
        norm(h)
    ps_next = heads[0][1](heads[0][3])
    for i, (_, _, values, h) in enumerate(heads):
        ps = ps_next
        if i + NORM_AHEAD < len(heads):
            norm_ahead, _, _, h_ahead = heads[i + NORM_AHEAD]
            norm_ahead(h_ahead)
        if i != MERGE_DRAIN_SLOT and pending:
            pending.pop(0)()
        if i >= CONV_FIRST_SLOT and conv_units:
            conv_units.pop(0)()
        if i + 1 < len(heads):
            _, scores_next, _, h_next = heads[i + 1]
            ps_next = scores_next(h_next)
        values(h, ps)
    assert not conv_units and not pending


def _mix_merge_call(proj, xf, mk, mv, bias, conv_w, q_g, k_g, mq_g, b_gate, w_a, w_b, w_m, w_o,
                    layer, *, seq, cols):
    n_tok, d = xf.shape
    t = STEP
    n_tiles = n_tok // t
    tiles_per_seq = seq // t
    n_heads = W_B // HEAD_DIM
    n_heads_m = W_M // HEAD_DIM
    mem_len = mk.shape[2]

    def mix_tile(s):
        return jnp.minimum(s, n_tiles - 1)

    def merge_tile(s):
        return jnp.maximum(s - 1, 0)

    def mix_spec(width, col):
        assert col % width == 0
        return pl.BlockSpec((t, width), lambda s: (mix_tile(s), col // width))

    assert cols["gate"] == 0
    resident = dict(pipeline_mode=pl.Buffered(1))
    in_specs = [
        mix_spec(4 * W_A, cols["conv"]), mix_spec(4 * W_B, cols["attn"]), mix_spec(2 * W_M, cols["mem"]),
        pl.BlockSpec((1, 1, mem_len, W_M), lambda s: (layer, mix_tile(s) // tiles_per_seq, 0, 0)),
        pl.BlockSpec((1, 1, n_heads_m, mem_len, 2 * HEAD_DIM),
                     lambda s: (layer, mix_tile(s) // tiles_per_seq, 0, 0, 0)),
        pl.BlockSpec((None, n_heads, GROUP_ROWS, WINDOW), lambda s: (layer, 0, 0, 0), **resident),
        pl.BlockSpec((None, CONV_WIDTH, W_A), lambda s: (layer, 0, 0)),
        pl.BlockSpec((None, 1, HEAD_DIM), lambda s: (layer, 0, 0)),
        pl.BlockSpec((None, 1, HEAD_DIM), lambda s: (layer, 0, 0)),
        pl.BlockSpec((None, 1, HEAD_DIM), lambda s: (layer, 0, 0)),
        pl.BlockSpec((t, 3 * d), lambda s: (merge_tile(s), 0)),
        pl.BlockSpec((t, d), lambda s: (merge_tile(s), 0)),
        pl.BlockSpec((None, 3, d), lambda s: (layer, 0, 0)),
        pl.BlockSpec((None, W_A, d), lambda s: (layer, 0, 0), **resident),
        pl.BlockSpec((None, W_B, d), lambda s: (layer, 0, 0), **resident),
        pl.BlockSpec((None, W_M, d), lambda s: (layer, 0, 0), **resident),
        pl.BlockSpec((None, d, d), lambda s: (layer, 0, 0), **resident),
    ]
    return pl.pallas_call(
        functools.partial(_mix_merge_kernel, n_tiles=n_tiles, tiles_per_seq=tiles_per_seq),
        grid=(n_tiles + 1,),
        in_specs=in_specs,
        out_specs=pl.BlockSpec((t, d), lambda s: (merge_tile(s), 0)),
        out_shape=jax.ShapeDtypeStruct((n_tok, d), F32),
        scratch_shapes=[
            pltpu.VMEM((t + CARRY_ROWS, W_A), F32),
            pltpu.VMEM((n_heads, HALO + t, KEY_LANES), BF16),
            pltpu.VMEM((n_heads, HALO + t, 2 * HEAD_DIM), BF16),
            pltpu.VMEM((n_heads, t, KEY_LANES), BF16),
            pltpu.VMEM((n_heads_m, t, HEAD_DIM), BF16),
            pltpu.VMEM((t, W_A + W_B + W_M), BF16),
            pltpu.VMEM((t, W_A + W_B + W_M), BF16),
            pltpu.VMEM((t, d), BF16),
        ],
        compiler_params=_params(1),
        name="mix_merge",
    )(proj, proj, proj, mk, mv, bias, conv_w, q_g, k_g, mq_g, proj, xf, b_gate, w_a, w_b, w_m, w_o)


def kernel(x, mem, norm_g, w_in, b_gate, conv_w, q_norm_g, k_norm_g, rel_table, mem_norm_g,
           w_mem_kv, mq_norm_g, mk_norm_g, w_branch_a, w_branch_b, w_branch_m, w_out):
    bsz, seq, d = x.shape
    n_layers = w_in.shape[0]
    n_tok = bsz * seq
    assert seq % HALO == 0 and HALO % STEP == 0 and d == 2 * GATE_BLOCK

    n_mix = 4 * W_A + 4 * W_B + 2 * W_M
    n_gate = 3 * d
    assert w_in.shape[-1] == n_mix + n_gate and n_gate % IN_COL_TILE == 0
    cols = {"gate": 0, "conv": n_gate, "attn": n_gate + 4 * W_A, "mem": n_gate + 4 * W_A + 4 * W_B}
    w_in_b = w_in.astype(BF16)
    w_kv = w_mem_kv.astype(BF16)
    w_a, w_b, w_m, w_o = (w.astype(BF16) for w in (w_branch_a, w_branch_b, w_branch_m, w_out))
    norm_g3 = norm_g.reshape(n_layers, 1, d)
    q_g = q_norm_g.reshape(n_layers, 1, HEAD_DIM)
    k_g = k_norm_g.reshape(n_layers, 1, HEAD_DIM)
    mq_g = mq_norm_g.reshape(n_layers, 1, HEAD_DIM)

    tm_in = min(IN_ROW_TILE, n_tok)
    tn_in = IN_COL_TILE

    bias = _bias_call(rel_table)
    mk, mv = _memkv_call(mem, mem_norm_g, w_kv, mk_norm_g)

    xf = x.reshape(n_tok, d)
    for layer in range(n_layers):
        proj = _inproj_call(xf, norm_g3, w_in_b, layer, tm=tm_in, tn=tn_in, col_shift=n_gate // tn_in)
        xf = _mix_merge_call(proj, xf, mk, mv, bias, conv_w, q_g, k_g, mq_g, b_gate,
                             w_a, w_b, w_m, w_o, layer, seq=seq, cols=cols)
    return xf.reshape(bsz, seq, d)
```

```python
import functools

import jax
import jax.numpy as jnp
from jax import lax
from jax.experimental import pallas as pl
from jax.experimental.pallas import tpu as pltpu

F32 = jnp.float32
BF16 = jnp.bfloat16

CHUNK = 64
N_PREV = 8
N_BAND = N_PREV + 1
HEAD_DIM = 128
W_A = 512
W_B = 1024
W_M = 512
CONV_WIDTH = 3
MAX_REL = 256
EPS = 1e-6
LOG2E = 1.4426950408889634
MASK_VALUE = -1e30

GROUP_CHUNKS = 2
GROUP_ROWS = GROUP_CHUNKS * CHUNK
WINDOW = (N_BAND + GROUP_CHUNKS - 1) * CHUNK
HALO = N_PREV * CHUNK
BIAS_LANES = 1024
LANES = 128
BIAS_CONST_LANES = (N_PREV * CHUNK + CHUNK - MAX_REL) // LANES * LANES

VMEM_LIMIT_BYTES = 58 * 1024 * 1024


def _params(n_axes):
    return pltpu.CompilerParams(
        dimension_semantics=("arbitrary",) * n_axes,
        vmem_limit_bytes=VMEM_LIMIT_BYTES,
    )


def _sigmoid(x):
    return 1.0 / (1.0 + jnp.exp2(x * -LOG2E))


def _silu(x):
    return x * _sigmoid(x)


def _rms(x, g):
    ms = jnp.mean(x * x, axis=-1, keepdims=True)
    return (x * lax.rsqrt(ms + EPS)) * g


def _bias_kernel(table_ref, o_ref, *, n_heads):
    l = pl.program_id(0)
    n_rel = 2 * MAX_REL + 1
    base = l * (n_rel * n_heads)
    lane = BIAS_CONST_LANES + lax.broadcasted_iota(jnp.int32, (8, WINDOW - BIAS_CONST_LANES), 1)
    rel = N_PREV * CHUNK - (lane - CHUNK)
    idx = jnp.clip(rel, -MAX_REL, MAX_REL) + MAX_REL
    lo = MAX_REL - (CHUNK - 1)

    def fill(r, accs):
        hit = idx == r
        return tuple(jnp.where(hit, table_ref[base + r * n_heads + h], acc) for h, acc in enumerate(accs))

    zeros = jnp.zeros((8, WINDOW - BIAS_CONST_LANES), F32)
    varying = lax.fori_loop(lo, n_rel, fill, (zeros,) * n_heads)
    col = lax.broadcasted_iota(jnp.int32, (CHUNK, WINDOW), 1)
    for h in range(n_heads):
        far = jnp.full((8, BIAS_CONST_LANES), table_ref[base + (n_rel - 1) * n_heads + h], F32)
        f_row = jnp.concatenate([far, varying[h], jnp.zeros((8, BIAS_LANES - WINDOW), F32)], axis=1)
        f_rows = jnp.broadcast_to(f_row[0:1, :] * LOG2E, (CHUNK, BIAS_LANES))
        r0 = pltpu.roll(f_rows, BIAS_LANES - CHUNK, 1, stride=1, stride_axis=0)[:, :WINDOW]
        o_ref[h, 0:CHUNK, :] = jnp.where(col < N_BAND * CHUNK, r0, MASK_VALUE)
        r1 = pltpu.roll(f_rows, 0, 1, stride=1, stride_axis=0)[:, :WINDOW]
        o_ref[h, CHUNK:GROUP_ROWS, :] = jnp.where(col >= CHUNK, r1, MASK_VALUE)


def _bias_call(rel_table):
    n_layers, n_rel, n_heads = rel_table.shape
    return pl.pallas_call(
        functools.partial(_bias_kernel, n_heads=n_heads),
        grid=(n_layers,),
        in_specs=[pl.BlockSpec(memory_space=pltpu.SMEM)],
        out_specs=pl.BlockSpec((None, n_heads, GROUP_ROWS, WINDOW), lambda l: (l, 0, 0, 0)),
        out_shape=jax.ShapeDtypeStruct((n_layers, n_heads, GROUP_ROWS, WINDOW), F32),
        compiler_params=_params(1),
        name="rel_bias",
    )(rel_table.reshape(-1))


def _memkv_kernel(mem_ref, g_ref, w_ref, kg_ref, mk_ref, mv_ref):
    h = _rms(mem_ref[0], g_ref[0]).astype(BF16)
    kv = jnp.dot(h, w_ref[0], preferred_element_type=F32)
    for hd in range(W_M // HEAD_DIM):
        cs = slice(hd * HEAD_DIM, (hd + 1) * HEAD_DIM)
        mk_ref[0, 0, :, cs] = _rms(kv[:, cs], kg_ref[0]).astype(BF16)
        mv_ref[0, 0, hd, :, 0:HEAD_DIM] = kv[:, W_M + hd * HEAD_DIM:W_M + (hd + 1) * HEAD_DIM].astype(BF16)
        mv_ref[0, 0, hd, :, HEAD_DIM:] = jnp.ones((kv.shape[0], HEAD_DIM), BF16)


def _memkv_call(mem, mem_norm_g, w_mem_kv, mk_norm_g):
    bsz, mem_len, d = mem.shape
    n_layers = w_mem_kv.shape[0]
    n_heads = W_M // HEAD_DIM
    return pl.pallas_call(
        _memkv_kernel,
        grid=(n_layers, bsz),
        in_specs=[
            pl.BlockSpec((1, mem_len, d), lambda l, b: (b, 0, 0)),
            pl.BlockSpec((1, 1, d), lambda l, b: (l, 0, 0)),
            pl.BlockSpec((1, d, 2 * W_M), lambda l, b: (l, 0, 0)),
            pl.BlockSpec((1, 1, HEAD_DIM), lambda l, b: (l, 0, 0)),
        ],
        out_specs=[
            pl.BlockSpec((1, 1, mem_len, W_M), lambda l, b: (l, b, 0, 0)),
            pl.BlockSpec((1, 1, n_heads, mem_len, 2 * HEAD_DIM), lambda l, b: (l, b, 0, 0, 0)),
        ],
        out_shape=[
            jax.ShapeDtypeStruct((n_layers, bsz, mem_len, W_M), BF16),
            jax.ShapeDtypeStruct((n_layers, bsz, n_heads, mem_len, 2 * HEAD_DIM), BF16),
        ],
        compiler_params=_params(2),
        name="mem_kv",
    )(mem, mem_norm_g.reshape(n_layers, 1, d), w_mem_kv, mk_norm_g.reshape(n_layers, 1, HEAD_DIM))


IN_ROW_TILE = 2048
IN_COL_TILE = 1024
NORM_ROWS = 128
NORM_CHUNK = 208
NORM_SUB = 16
DOT_ROWS = 512
FIRST_HALF_STEP = 2
SECOND_HALF_STEP = 7


def _inproj_kernel(x_hbm, g_ref, w_ref, o_ref, xbuf, h_even, h_odd, sem, *, n_row_tiles, n_col_tiles):
    i = pl.program_id(0)
    j = pl.program_id(1)
    tm = xbuf.shape[0]
    half = tm // 2
    has_next = i + 1 < n_row_tiles
    assert (SECOND_HALF_STEP - FIRST_HALF_STEP) * NORM_CHUNK >= half
    assert (n_col_tiles - SECOND_HALF_STEP) * NORM_CHUNK >= half
    assert NORM_CHUNK % NORM_SUB == 0 and half % NORM_SUB == 0 and NORM_CHUNK <= half

    def x_copy(tile, part):
        return pltpu.make_async_copy(
            x_hbm.at[pl.ds(tile * tm + part * half, half), :],
            xbuf.at[pl.ds(part * half, half), :],
            sem.at[part])

    @pl.when((i == 0) & (j == 0))
    def _():
        x_copy(0, 0).start()
        x_copy(0, 1).start()
        x_copy(0, 0).wait()
        x_copy(0, 1).wait()

        def body(c, carry):
            r = pl.multiple_of(c * NORM_ROWS, NORM_ROWS)
            h_even[pl.ds(r, NORM_ROWS), :] = _rms(xbuf[pl.ds(r, NORM_ROWS), :], g_ref[0]).astype(BF16)
            return carry

        lax.fori_loop(0, tm // NORM_ROWS, body, 0)

    @pl.when(has_next & (j == 0))
    def _():
        x_copy(i + 1, 0).start()

    @pl.when(has_next & (j == FIRST_HALF_STEP))
    def _():
        x_copy(i + 1, 0).wait()

    @pl.when(has_next & (j == SECOND_HALF_STEP - 1))
    def _():
        x_copy(i + 1, 1).start()

    @pl.when(has_next & (j == SECOND_HALF_STEP))
    def _():
        x_copy(i + 1, 1).wait()

    last = half - NORM_CHUNK
    start = jnp.where(
        j < FIRST_HALF_STEP, half + last,
        jnp.where(j < SECOND_HALF_STEP,
                  jnp.minimum((j - FIRST_HALF_STEP) * NORM_CHUNK, last),
                  half + jnp.minimum((j - SECOND_HALF_STEP) * NORM_CHUNK, last)))
    start = pl.multiple_of(start, NORM_SUB)

    def step(h_cur, h_next):
        for k in range(0, NORM_CHUNK, NORM_SUB):
            rows = pl.ds(start + k, NORM_SUB)
            h_next[rows, :] = _rms(xbuf[rows, :], g_ref[0]).astype(BF16)
        dot_rows = min(DOT_ROWS, tm)
        for r in range(0, tm, dot_rows):
            o_ref[r:r + dot_rows, :] = jnp.dot(h_cur[r:r + dot_rows, :], w_ref[0],
                                               preferred_element_type=F32).astype(o_ref.dtype)

    @pl.when(i % 2 == 0)
    def _():
        step(h_even, h_odd)

    @pl.when(i % 2 == 1)
    def _():
        step(h_odd, h_even)


def _inproj_call(xf, norm_g, w_in, layer, *, tm, tn, col_shift):
    m, d = xf.shape
    n = w_in.shape[-1]
    n_col = n // tn
    return pl.pallas_call(
        functools.partial(_inproj_kernel, n_row_tiles=m // tm, n_col_tiles=n // tn),
        grid=(m // tm, n // tn),
        in_specs=[
            pl.BlockSpec(memory_space=pl.ANY),
            pl.BlockSpec((1, 1, d), lambda i, j: (layer, 0, 0)),
            pl.BlockSpec((1, d, tn), lambda i, j: (layer, 0, j)),
        ],
        out_specs=pl.BlockSpec((tm, tn), lambda i, j: (i, (j + col_shift) % n_col)),
        out_shape=jax.ShapeDtypeStruct((m, n), BF16),
        scratch_shapes=[
            pltpu.VMEM((tm, d), F32),
            pltpu.VMEM((tm, d), BF16),
            pltpu.VMEM((tm, d), BF16),
            pltpu.SemaphoreType.DMA((2,)),
        ],
        compiler_params=_params(2),
        name="in_proj",
    )(xf, norm_g, w_in)


STEP = 256
STEP_GROUPS = STEP // GROUP_ROWS
CARRY_ROWS = 8
MEM_ROWS = 128
MERGE_COLS = 512
GATE_BLOCK = 1024
KEY_LANES = 2 * HEAD_DIM
OUT_COLS = 256
NORM_AHEAD = 2
MERGE_DRAIN_SLOT = 3
CONV_ROWS = 128
CONV_FIRST_SLOT = 8

def _mix_merge_kernel(conv_ref, attn_ref, memq_ref, mk_ref, mv_ref, bias_ref, cw_ref, qg_ref, kg_ref, mqg_ref,
                      gate_ref, x_ref, bg_ref, wa_ref, wb_ref, wm_ref, wo_ref,
                      o_ref, u_scr, kw_scr, vw_scr, qn_scr, mqn_scr, y_scr, yp_scr, m_scr, *, n_tiles, tiles_per_seq):
    ab_ref, ac_ref, ax_ref, az_ref = (conv_ref.at[:, j * W_A:(j + 1) * W_A] for j in range(4))
    q_ref, k_ref, v_ref, bz_ref = (attn_ref.at[:, j * W_B:(j + 1) * W_B] for j in range(4))
    mq_ref, mz_ref = (memq_ref.at[:, j * W_M:(j + 1) * W_M] for j in range(2))
    ga0, ga1, gb0, gb1, gm0, gm1 = (gate_ref.at[:, j * GATE_BLOCK:(j + 1) * GATE_BLOCK] for j in range(6))
    t = STEP
    n_heads_b = W_B // HEAD_DIM
    n_heads_m = W_M // HEAD_DIM
    step = pl.program_id(0)
    tile = jnp.minimum(step, n_tiles - 1)
    seq_start = tile % tiles_per_seq == 0
    scale = HEAD_DIM ** -0.5 * LOG2E

    @pl.when(step == 0)
    def _():
        one_hot = jnp.where(lax.broadcasted_iota(jnp.int32, (t, HEAD_DIM), 1) == 0, 1.0, 0.0).astype(BF16)
        y_scr[...] = jnp.zeros(y_scr.shape, BF16)
        u_scr[...] = jnp.zeros(u_scr.shape, F32)
        for h in range(n_heads_b):
            kw_scr[h] = jnp.zeros(kw_scr.shape[1:], BF16)
            vw_scr[h] = jnp.zeros(vw_scr.shape[1:], BF16)
            qn_scr[h, :, HEAD_DIM:] = one_hot

    yp_scr[...] = y_scr[...]

    @pl.when(seq_start)
    def _():
        lane = lax.broadcasted_iota(jnp.int32, (HALO, KEY_LANES), 1)
        pen = jnp.where(lane == HEAD_DIM, MASK_VALUE, 0.0).astype(BF16)
        u_scr[t:t + CARRY_ROWS, :] = jnp.zeros((CARRY_ROWS, W_A), F32)
        for h in range(n_heads_b):
            kw_scr[h, t:, :] = pen
            vw_scr[h, t:, :] = jnp.zeros((HALO, KEY_LANES), BF16)

    d = o_ref.shape[1]
    gates = ((ga0, ga1), (gb0, gb1), (gm0, gm1))

    def conv_taps():
        u_scr[0:CARRY_ROWS, :] = u_scr[t:t + CARRY_ROWS, :]
        u_scr[CARRY_ROWS:CARRY_ROWS + t, :] = ac_ref[...].astype(F32) * ax_ref[...].astype(F32)

    def conv_rows(r0):
        rs = slice(r0, r0 + CONV_ROWS)
        y = u_scr[CARRY_ROWS - 2 + r0:CARRY_ROWS - 2 + r0 + CONV_ROWS, :] * cw_ref[0:1, :]
        y = y + u_scr[CARRY_ROWS - 1 + r0:CARRY_ROWS - 1 + r0 + CONV_ROWS, :] * cw_ref[1:2, :]
        y = y + u_scr[CARRY_ROWS + r0:CARRY_ROWS + r0 + CONV_ROWS, :] * cw_ref[2:3, :]
        ya = (ab_ref[rs, :].astype(F32) * y) * _silu(az_ref[rs, :].astype(F32))
        y_scr[rs, 0:W_A] = ya.astype(BF16)

    def band_norm(h):
        hs = slice(h * HEAD_DIM, (h + 1) * HEAD_DIM)
        for r in range(0, HALO, t):
            kw_scr[h, r:r + t, :] = kw_scr[h, r + t:r + 2 * t, :]
            vw_scr[h, r:r + t, :] = vw_scr[h, r + t:r + 2 * t, :]
        kw_scr[h, HALO:, 0:HEAD_DIM] = _rms(k_ref[:, hs].astype(F32), kg_ref[...]).astype(BF16)
        kw_scr[h, HALO:, HEAD_DIM:] = jnp.zeros((t, HEAD_DIM), BF16)
        vw_scr[h, HALO:, 0:HEAD_DIM] = v_ref[:, hs]
        vw_scr[h, HALO:, HEAD_DIM:] = jnp.ones((t, HEAD_DIM), BF16)
        qn_scr[h, :, 0:HEAD_DIM] = _rms(q_ref[:, hs].astype(F32), qg_ref[...] * scale).astype(BF16)

    def band_scores(h):
        ps = []
        for g in range(STEP_GROUPS):
            r0 = g * GROUP_ROWS
            s = lax.dot_general(qn_scr[h, r0:r0 + GROUP_ROWS, :], kw_scr[h, r0:r0 + WINDOW, :],
                                (((1,), (1,)), ((), ())), preferred_element_type=F32)
            s = s + bias_ref[h]
            ps.append(jnp.exp2(s - jnp.max(s, axis=-1, keepdims=True)).astype(BF16))
        return ps

    def band_values(h, ps):
        hs = slice(h * HEAD_DIM, (h + 1) * HEAD_DIM)
        for g, p in enumerate(ps):
            r0 = g * GROUP_ROWS
            rs = slice(r0, r0 + GROUP_ROWS)
            od = jnp.dot(p, vw_scr[h, r0:r0 + WINDOW, :], preferred_element_type=F32)
            o = od[:, 0:HEAD_DIM] / od[:, HEAD_DIM:]
            y_scr[rs, W_A + h * HEAD_DIM:W_A + (h + 1) * HEAD_DIM] = (
                o * _silu(bz_ref[rs, hs].astype(F32))).astype(BF16)

    def mem_norm(h):
        hs = slice(h * HEAD_DIM, (h + 1) * HEAD_DIM)
        mqn_scr[h] = _rms(mq_ref[:, hs].astype(F32), mqg_ref[...] * scale).astype(BF16)

    def mem_scores(h):
        mk = mk_ref[0, 0, :, h * HEAD_DIM:(h + 1) * HEAD_DIM]
        ps = []
        for r0 in range(0, t, MEM_ROWS):
            s = lax.dot_general(mqn_scr[h, r0:r0 + MEM_ROWS, :], mk, (((1,), (1,)), ((), ())),
                                preferred_element_type=F32)
            ps.append(jnp.exp2(s - jnp.max(s, axis=-1, keepdims=True)).astype(BF16))
        return ps

    def mem_values(h, ps):
        hs = slice(h * HEAD_DIM, (h + 1) * HEAD_DIM)
        mv = mv_ref[0, 0, h]
        for i, p in enumerate(ps):
            rs = slice(i * MEM_ROWS, (i + 1) * MEM_ROWS)
            od = jnp.dot(p, mv, preferred_element_type=F32)
            o = od[:, 0:HEAD_DIM] / od[:, HEAD_DIM:]
            c = W_A + W_B + h * HEAD_DIM
            y_scr[rs, c:c + HEAD_DIM] = (o * _silu(mz_ref[rs, hs].astype(F32))).astype(BF16)

    def merge(j):
        c = j * MERGE_COLS
        cs = slice(c, c + MERGE_COLS)
        gs = slice(c % GATE_BLOCK, c % GATE_BLOCK + MERGE_COLS)
        blk = c // GATE_BLOCK
        pa = jnp.dot(yp_scr[:, 0:W_A], wa_ref[:, cs], preferred_element_type=F32)
        acc = _sigmoid(gates[0][blk][:, gs].astype(F32) + bg_ref[0:1, cs]) * pa
        pb = jnp.dot(yp_scr[:, W_A:W_A + W_B], wb_ref[:, cs], preferred_element_type=F32)
        acc = acc + _sigmoid(gates[1][blk][:, gs].astype(F32) + bg_ref[1:2, cs]) * pb
        pm = jnp.dot(yp_scr[:, W_A + W_B:], wm_ref[:, cs], preferred_element_type=F32)
        acc = acc + _sigmoid(gates[2][blk][:, gs].astype(F32) + bg_ref[2:3, cs]) * pm
        m_scr[:, cs] = acc.astype(BF16)

    def out(j):
        cs = slice(j * OUT_COLS, (j + 1) * OUT_COLS)
        o_ref[:, cs] = x_ref[:, cs] + jnp.dot(m_scr[...], wo_ref[:, cs], preferred_element_type=F32)

    heads = ([(band_norm, band_scores, band_values, h) for h in range(n_heads_b)]
             + [(mem_norm, mem_scores, mem_values, h) for h in range(n_heads_m)])
    fillers = ([functools.partial(merge, j) for j in range(d // MERGE_COLS)]
               + [functools.partial(out, j) for j in range(d // OUT_COLS)])
    assert len(fillers) == len(heads)
    conv_units = [conv_taps] + [functools.partial(conv_rows, r0) for r0 in range(0, t, CONV_ROWS)]
    pending = list(fillers)
    pending.pop(0)()
    for norm, _, _, h in heads[:NORM_AHEAD]:
        norm(h)
    ps_next = heads[0][1](heads[0][3])
    for i, (_, _, values, h) in enumerate(heads):
        ps = ps_next
        if i + NORM_AHEAD < len(heads):
            norm_ahead, _, _, h_ahead = heads[i + NORM_AHEAD]
            norm_ahead(h_ahead)
        if i != MERGE_DRAIN_SLOT and pending:
            pending.pop(0)()
        if i >= CONV_FIRST_SLOT and conv_units:
            conv_units.pop(0)()
        if i + 1 < len(heads):
            _, scores_next, _, h_next = heads[i + 1]
            ps_next = scores_next(h_next)
        values(h, ps)
    assert not conv_units and not pending


def _mix_merge_call(proj, xf, mk, mv, bias, conv_w, q_g, k_g, mq_g, b_gate, w_a, w_b, w_m, w_o,
                    layer, *, seq, cols):
    n_tok, d = xf.shape
    t = STEP
    n_tiles = n_tok // t
    tiles_per_seq = seq // t
    n_heads = W_B // HEAD_DIM
    n_heads_m = W_M // HEAD_DIM
    mem_len = mk.shape[2]

    def mix_tile(s):
        return jnp.minimum(s, n_tiles - 1)

    def merge_tile(s):
        return jnp.maximum(s - 1, 0)

    def mix_spec(width, col):
        assert col % width == 0
        return pl.BlockSpec((t, width), lambda s: (mix_tile(s), col // width))

    assert cols["gate"] == 0
    resident = dict(pipeline_mode=pl.Buffered(1))
    in_specs = [
        mix_spec(4 * W_A, cols["conv"]), mix_spec(4 * W_B, cols["attn"]), mix_spec(2 * W_M, cols["mem"]),
        pl.BlockSpec((1, 1, mem_len, W_M), lambda s: (layer, mix_tile(s) // tiles_per_seq, 0, 0)),
        pl.BlockSpec((1, 1, n_heads_m, mem_len, 2 * HEAD_DIM),
                     lambda s: (layer, mix_tile(s) // tiles_per_seq, 0, 0, 0)),
        pl.BlockSpec((None, n_heads, GROUP_ROWS, WINDOW), lambda s: (layer, 0, 0, 0), **resident),
        pl.BlockSpec((None, CONV_WIDTH, W_A), lambda s: (layer, 0, 0)),
        pl.BlockSpec((None, 1, HEAD_DIM), lambda s: (layer, 0, 0)),
        pl.BlockSpec((None, 1, HEAD_DIM), lambda s: (layer, 0, 0)),
        pl.BlockSpec((None, 1, HEAD_DIM), lambda s: (layer, 0, 0)),
        pl.BlockSpec((t, 3 * d), lambda s: (merge_tile(s), 0)),
        pl.BlockSpec((t, d), lambda s: (merge_tile(s), 0)),
        pl.BlockSpec((None, 3, d), lambda s: (layer, 0, 0)),
        pl.BlockSpec((None, W_A, d), lambda s: (layer, 0, 0), **resident),
        pl.BlockSpec((None, W_B, d), lambda s: (layer, 0, 0), **resident),
        pl.BlockSpec((None, W_M, d), lambda s: (layer, 0, 0), **resident),
        pl.BlockSpec((None, d, d), lambda s: (layer, 0, 0), **resident),
    ]
    return pl.pallas_call(
        functools.partial(_mix_merge_kernel, n_tiles=n_tiles, tiles_per_seq=tiles_per_seq),
        grid=(n_tiles + 1,),
        in_specs=in_specs,
        out_specs=pl.BlockSpec((t, d), lambda s: (merge_tile(s), 0)),
        out_shape=jax.ShapeDtypeStruct((n_tok, d), F32),
        scratch_shapes=[
            pltpu.VMEM((t + CARRY_ROWS, W_A), F32),
            pltpu.VMEM((n_heads, HALO + t, KEY_LANES), BF16),
            pltpu.VMEM((n_heads, HALO + t, 2 * HEAD_DIM), BF16),
            pltpu.VMEM((n_heads, t, KEY_LANES), BF16),
            pltpu.VMEM((n_heads_m, t, HEAD_DIM), BF16),
            pltpu.VMEM((t, W_A + W_B + W_M), BF16),
            pltpu.VMEM((t, W_A + W_B + W_M), BF16),
            pltpu.VMEM((t, d), BF16),
        ],
        compiler_params=_params(1),
        name="mix_merge",
    )(proj, proj, proj, mk, mv, bias, conv_w, q_g, k_g, mq_g, proj, xf, b_gate, w_a, w_b, w_m, w_o)


def kernel(x, mem, norm_g, w_in, b_gate, conv_w, q_norm_g, k_norm_g, rel_table, mem_norm_g,
           w_mem_kv, mq_norm_g, mk_norm_g, w_branch_a, w_branch_b, w_branch_m, w_out):
    bsz, seq, d = x.shape
    n_layers = w_in.shape[0]
    n_tok = bsz * seq
    assert seq % HALO == 0 and HALO % STEP == 0 and d == 2 * GATE_BLOCK

    n_mix = 4 * W_A + 4 * W_B + 2 * W_M
    n_gate = 3 * d
    assert w_in.shape[-1] == n_mix + n_gate and n_gate % IN_COL_TILE == 0
    cols = {"gate": 0, "conv": n_gate, "attn": n_gate + 4 * W_A, "mem": n_gate + 4 * W_A + 4 * W_B}
    w_in_b = w_in.astype(BF16)
    w_kv = w_mem_kv.astype(BF16)
    w_a, w_b, w_m, w_o = (w.astype(BF16) for w in (w_branch_a, w_branch_b, w_branch_m, w_out))
    norm_g3 = norm_g.reshape(n_layers, 1, d)
    q_g = q_norm_g.reshape(n_layers, 1, HEAD_DIM)
    k_g = k_norm_g.reshape(n_layers, 1, HEAD_DIM)
    mq_g = mq_norm_g.reshape(n_layers, 1, HEAD_DIM)

    tm_in = min(IN_ROW_TILE, n_tok)
    tn_in = IN_COL_TILE

    bias = _bias_call(rel_table)
    mk, mv = _memkv_call(mem, mem_norm_g, w_kv, mk_norm_g)

    xf = x.reshape(n_tok, d)
    for layer in range(n_layers):
        proj = _inproj_call(xf, norm_g3, w_in_b, layer, tm=tm_in, tn=tn_in, col_shift=n_gate // tn_in)
        xf = _mix_merge_call(proj, xf, mk, mv, bias, conv_w, q_g, k_g, mq_g, b_gate,
                             w_a, w_b, w_m, w_o, layer, seq=seq, cols=cols)
    return xf.reshape(bsz, seq, d)
```

```python
import functools

import jax
import jax.numpy as jnp
from jax import lax
from jax.experimental import pallas as pl
from jax.experimental.pallas import tpu as pltpu

F32 = jnp.float32
BF16 = jnp.bfloat16

CHUNK = 64
N_PREV = 8
N_BAND = N_PREV + 1
HEAD_DIM = 128
W_A = 512
W_B = 1024
W_M = 512
CONV_WIDTH = 3
MAX_REL = 256
EPS = 1e-6
LOG2E = 1.4426950408889634
MASK_VALUE = -1e30

GROUP_CHUNKS = 2
GROUP_ROWS = GROUP_CHUNKS * CHUNK
WINDOW = (N_BAND + GROUP_CHUNKS - 1) * CHUNK
HALO = N_PREV * CHUNK
BIAS_LANES = 1024
LANES = 128
BIAS_CONST_LANES = (N_PREV * CHUNK + CHUNK - MAX_REL) // LANES * LANES

VMEM_LIMIT_BYTES = 58 * 1024 * 1024


def _params(n_axes):
    return pltpu.CompilerParams(
        dimension_semantics=("arbitrary",) * n_axes,
        vmem_limit_bytes=VMEM_LIMIT_BYTES,
    )


def _sigmoid(x):
    return 1.0 / (1.0 + jnp.exp2(x * -LOG2E))


def _silu(x):
    return x * _sigmoid(x)


def _rms(x, g):
    ms = jnp.mean(x * x, axis=-1, keepdims=True)
    return (x * lax.rsqrt(ms + EPS)) * g


def _bias_kernel(table_ref, o_ref, *, n_heads):
    l = pl.program_id(0)
    n_rel = 2 * MAX_REL + 1
    base = l * (n_rel * n_heads)
    lane = BIAS_CONST_LANES + lax.broadcasted_iota(jnp.int32, (8, WINDOW - BIAS_CONST_LANES), 1)
    rel = N_PREV * CHUNK - (lane - CHUNK)
    idx = jnp.clip(rel, -MAX_REL, MAX_REL) + MAX_REL
    lo = MAX_REL - (CHUNK - 1)

    def fill(r, accs):
        hit = idx == r
        return tuple(jnp.where(hit, table_ref[base + r * n_heads + h], acc) for h, acc in enumerate(accs))

    zeros = jnp.zeros((8, WINDOW - BIAS_CONST_LANES), F32)
    varying = lax.fori_loop(lo, n_rel, fill, (zeros,) * n_heads)
    col = lax.broadcasted_iota(jnp.int32, (CHUNK, WINDOW), 1)
    for h in range(n_heads):
        far = jnp.full((8, BIAS_CONST_LANES), table_ref[base + (n_rel - 1) * n_heads + h], F32)
        f_row = jnp.concatenate([far, varying[h], jnp.zeros((8, BIAS_LANES - WINDOW), F32)], axis=1)
        f_rows = jnp.broadcast_to(f_row[0:1, :] * LOG2E, (CHUNK, BIAS_LANES))
        r0 = pltpu.roll(f_rows, BIAS_LANES - CHUNK, 1, stride=1, stride_axis=0)[:, :WINDOW]
        o_ref[h, 0:CHUNK, :] = jnp.where(col < N_BAND * CHUNK, r0, MASK_VALUE)
        r1 = pltpu.roll(f_rows, 0, 1, stride=1, stride_axis=0)[:, :WINDOW]
        o_ref[h, CHUNK:GROUP_ROWS, :] = jnp.where(col >= CHUNK, r1, MASK_VALUE)


def _bias_call(rel_table):
    n_layers, n_rel, n_heads = rel_table.shape
    return pl.pallas_call(
        functools.partial(_bias_kernel, n_heads=n_heads),
        grid=(n_layers,),
        in_specs=[pl.BlockSpec(memory_space=pltpu.SMEM)],
        out_specs=pl.BlockSpec((None, n_heads, GROUP_ROWS, WINDOW), lambda l: (l, 0, 0, 0)),
        out_shape=jax.ShapeDtypeStruct((n_layers, n_heads, GROUP_ROWS, WINDOW), F32),
        compiler_params=_params(1),
        name="rel_bias",
    )(rel_table.reshape(-1))


def _memkv_kernel(mem_ref, g_ref, w_ref, kg_ref, mk_ref, mv_ref):
    h = _rms(mem_ref[0], g_ref[0]).astype(BF16)
    kv = jnp.dot(h, w_ref[0], preferred_element_type=F32)
    for hd in range(W_M // HEAD_DIM):
        cs = slice(hd * HEAD_DIM, (hd + 1) * HEAD_DIM)
        mk_ref[0, 0, :, cs] = _rms(kv[:, cs], kg_ref[0]).astype(BF16)
        mv_ref[0, 0, hd, :, 0:HEAD_DIM] = kv[:, W_M + hd * HEAD_DIM:W_M + (hd + 1) * HEAD_DIM].astype(BF16)
        mv_ref[0, 0, hd, :, HEAD_DIM:] = jnp.ones((kv.shape[0], HEAD_DIM), BF16)


def _memkv_call(mem, mem_norm_g, w_mem_kv, mk_norm_g):
    bsz, mem_len, d = mem.shape
    n_layers = w_mem_kv.shape[0]
    n_heads = W_M // HEAD_DIM
    return pl.pallas_call(
        _memkv_kernel,
        grid=(n_layers, bsz),
        in_specs=[
            pl.BlockSpec((1, mem_len, d), lambda l, b: (b, 0, 0)),
            pl.BlockSpec((1, 1, d), lambda l, b: (l, 0, 0)),
            pl.BlockSpec((1, d, 2 * W_M), lambda l, b: (l, 0, 0)),
            pl.BlockSpec((1, 1, HEAD_DIM), lambda l, b: (l, 0, 0)),
        ],
        out_specs=[
            pl.BlockSpec((1, 1, mem_len, W_M), lambda l, b: (l, b, 0, 0)),
            pl.BlockSpec((1, 1, n_heads, mem_len, 2 * HEAD_DIM), lambda l, b: (l, b, 0, 0, 0)),
        ],
        out_shape=[
            jax.ShapeDtypeStruct((n_layers, bsz, mem_len, W_M), BF16),
            jax.ShapeDtypeStruct((n_layers, bsz, n_heads, mem_len, 2 * HEAD_DIM), BF16),
        ],
        compiler_params=_params(2),
        name="mem_kv",
    )(mem, mem_norm_g.reshape(n_layers, 1, d), w_mem_kv, mk_norm_g.reshape(n_layers, 1, HEAD_DIM))


IN_ROW_TILE = 2048
IN_COL_TILE = 1024
NORM_ROWS = 128
NORM_CHUNK = 208
NORM_SUB = 16
DOT_ROWS = 512
FIRST_HALF_STEP = 2
SECOND_HALF_STEP = 7


def _inproj_kernel(x_hbm, g_ref, w_ref, o_ref, xbuf, h_even, h_odd, sem, *, n_row_tiles, n_col_tiles):
    i = pl.program_id(0)
    j = pl.program_id(1)
    tm = xbuf.shape[0]
    half = tm // 2
    has_next = i + 1 < n_row_tiles
    assert (SECOND_HALF_STEP - FIRST_HALF_STEP) * NORM_CHUNK >= half
    assert (n_col_tiles - SECOND_HALF_STEP) * NORM_CHUNK >= half
    assert NORM_CHUNK % NORM_SUB == 0 and half % NORM_SUB == 0 and NORM_CHUNK <= half

    def x_copy(tile, part):
        return pltpu.make_async_copy(
            x_hbm.at[pl.ds(tile * tm + part * half, half), :],
            xbuf.at[pl.ds(part * half, half), :],
            sem.at[part])

    @pl.when((i == 0) & (j == 0))
    def _():
        x_copy(0, 0).start()
        x_copy(0, 1).start()
        x_copy(0, 0).wait()
        x_copy(0, 1).wait()

        def body(c, carry):
            r = pl.multiple_of(c * NORM_ROWS, NORM_ROWS)
            h_even[pl.ds(r, NORM_ROWS), :] = _rms(xbuf[pl.ds(r, NORM_ROWS), :], g_ref[0]).astype(BF16)
            return carry

        lax.fori_loop(0, tm // NORM_ROWS, body, 0)

    @pl.when(has_next & (j == 0))
    def _():
        x_copy(i + 1, 0).start()

    @pl.when(has_next & (j == FIRST_HALF_STEP))
    def _():
        x_copy(i + 1, 0).wait()

    @pl.when(has_next & (j == SECOND_HALF_STEP - 1))
    def _():
        x_copy(i + 1, 1).start()

    @pl.when(has_next & (j == SECOND_HALF_STEP))
    def _():
        x_copy(i + 1, 1).wait()

    last = half - NORM_CHUNK
    start = jnp.where(
        j < FIRST_HALF_STEP, half + last,
        jnp.where(j < SECOND_HALF_STEP,
                  jnp.minimum((j - FIRST_HALF_STEP) * NORM_CHUNK, last),
                  half + jnp.minimum((j - SECOND_HALF_STEP) * NORM_CHUNK, last)))
    start = pl.multiple_of(start, NORM_SUB)

    def step(h_cur, h_next):
        for k in range(0, NORM_CHUNK, NORM_SUB):
            rows = pl.ds(start + k, NORM_SUB)
            h_next[rows, :] = _rms(xbuf[rows, :], g_ref[0]).astype(BF16)
        dot_rows = min(DOT_ROWS, tm)
        for r in range(0, tm, dot_rows):
            o_ref[r:r + dot_rows, :] = jnp.dot(h_cur[r:r + dot_rows, :], w_ref[0],
                                               preferred_element_type=F32).astype(o_ref.dtype)

    @pl.when(i % 2 == 0)
    def _():
        step(h_even, h_odd)

    @pl.when(i % 2 == 1)
    def _():
        step(h_odd, h_even)


def _inproj_call(xf, norm_g, w_in, layer, *, tm, tn, col_shift):
    m, d = xf.shape
    n = w_in.shape[-1]
    n_col = n // tn
    return pl.pallas_call(
        functools.partial(_inproj_kernel, n_row_tiles=m // tm, n_col_tiles=n // tn),
        grid=(m // tm, n // tn),
        in_specs=[
            pl.BlockSpec(memory_space=pl.ANY),
            pl.BlockSpec((1, 1, d), lambda i, j: (layer, 0, 0)),
            pl.BlockSpec((1, d, tn), lambda i, j: (layer, 0, j)),
        ],
        out_specs=pl.BlockSpec((tm, tn), lambda i, j: (i, (j + col_shift) % n_col)),
        out_shape=jax.ShapeDtypeStruct((m, n), BF16),
        scratch_shapes=[
            pltpu.VMEM((tm, d), F32),
            pltpu.VMEM((tm, d), BF16),
            pltpu.VMEM((tm, d), BF16),
            pltpu.SemaphoreType.DMA((2,)),
        ],
        compiler_params=_params(2),
        name="in_proj",
    )(xf, norm_g, w_in)


STEP = 256
STEP_GROUPS = STEP // GROUP_ROWS
CARRY_ROWS = 8
MEM_ROWS = 128
MERGE_COLS = 512
GATE_BLOCK = 1024
KEY_LANES = 2 * HEAD_DIM
OUT_COLS = 256
NORM_AHEAD = 2
CONV_ROWS = 128
CONV_FIRST_SLOT = 8

def _mix_merge_kernel(conv_ref, attn_ref, memq_ref, mk_ref, mv_ref, bias_ref, cw_ref, qg_ref, kg_ref, mqg_ref,
                      gate_ref, x_ref, bg_ref, wa_ref, wb_ref, wm_ref, wo_ref,
                      o_ref, u_scr, kw_scr, vw_scr, qn_scr, mqn_scr, y_scr, yp_scr, m_scr, mp_scr, *, n_tiles, tiles_per_seq):
    ab_ref, ac_ref, ax_ref, az_ref = (conv_ref.at[:, j * W_A:(j + 1) * W_A] for j in range(4))
    q_ref, k_ref, v_ref, bz_ref = (attn_ref.at[:, j * W_B:(j + 1) * W_B] for j in range(4))
    mq_ref, mz_ref = (memq_ref.at[:, j * W_M:(j + 1) * W_M] for j in range(2))
    ga0, ga1, gb0, gb1, gm0, gm1 = (gate_ref.at[:, j * GATE_BLOCK:(j + 1) * GATE_BLOCK] for j in range(6))
    t = STEP
    n_heads_b = W_B // HEAD_DIM
    n_heads_m = W_M // HEAD_DIM
    step = pl.program_id(0)
    tile = jnp.minimum(step, n_tiles - 1)
    seq_start = tile % tiles_per_seq == 0
    scale = HEAD_DIM ** -0.5 * LOG2E

    @pl.when(step == 0)
    def _():
        one_hot = jnp.where(lax.broadcasted_iota(jnp.int32, (t, HEAD_DIM), 1) == 0, 1.0, 0.0).astype(BF16)
        y_scr[...] = jnp.zeros(y_scr.shape, BF16)
        m_scr[...] = jnp.zeros(m_scr.shape, BF16)
        u_scr[...] = jnp.zeros(u_scr.shape, F32)
        for h in range(n_heads_b):
            kw_scr[h] = jnp.zeros(kw_scr.shape[1:], BF16)
            vw_scr[h] = jnp.zeros(vw_scr.shape[1:], BF16)
            qn_scr[h, :, HEAD_DIM:] = one_hot

    yp_scr[...] = y_scr[...]
    mp_scr[...] = m_scr[...]

    @pl.when(seq_start)
    def _():
        lane = lax.broadcasted_iota(jnp.int32, (HALO, KEY_LANES), 1)
        pen = jnp.where(lane == HEAD_DIM, MASK_VALUE, 0.0).astype(BF16)
        u_scr[t:t + CARRY_ROWS, :] = jnp.zeros((CARRY_ROWS, W_A), F32)
        for h in range(n_heads_b):
            kw_scr[h, t:, :] = pen
            vw_scr[h, t:, :] = jnp.zeros((HALO, KEY_LANES), BF16)

    d = o_ref.shape[1]
    gates = ((ga0, ga1), (gb0, gb1), (gm0, gm1))

    def conv_taps():
        u_scr[0:CARRY_ROWS, :] = u_scr[t:t + CARRY_ROWS, :]
        u_scr[CARRY_ROWS:CARRY_ROWS + t, :] = ac_ref[...].astype(F32) * ax_ref[...].astype(F32)

    def conv_rows(r0):
        rs = slice(r0, r0 + CONV_ROWS)
        y = u_scr[CARRY_ROWS - 2 + r0:CARRY_ROWS - 2 + r0 + CONV_ROWS, :] * cw_ref[0:1, :]
        y = y + u_scr[CARRY_ROWS - 1 + r0:CARRY_ROWS - 1 + r0 + CONV_ROWS, :] * cw_ref[1:2, :]
        y = y + u_scr[CARRY_ROWS + r0:CARRY_ROWS + r0 + CONV_ROWS, :] * cw_ref[2:3, :]
        ya = (ab_ref[rs, :].astype(F32) * y) * _silu(az_ref[rs, :].astype(F32))
        y_scr[rs, 0:W_A] = ya.astype(BF16)

    def band_norm(h):
        hs = slice(h * HEAD_DIM, (h + 1) * HEAD_DIM)
        for r in range(0, HALO, t):
            kw_scr[h, r:r + t, :] = kw_scr[h, r + t:r + 2 * t, :]
            vw_scr[h, r:r + t, :] = vw_scr[h, r + t:r + 2 * t, :]
        kw_scr[h, HALO:, 0:HEAD_DIM] = _rms(k_ref[:, hs].astype(F32), kg_ref[...]).astype(BF16)
        kw_scr[h, HALO:, HEAD_DIM:] = jnp.zeros((t, HEAD_DIM), BF16)
        vw_scr[h, HALO:, 0:HEAD_DIM] = v_ref[:, hs]
        vw_scr[h, HALO:, HEAD_DIM:] = jnp.ones((t, HEAD_DIM), BF16)
        qn_scr[h, :, 0:HEAD_DIM] = _rms(q_ref[:, hs].astype(F32), qg_ref[...] * scale).astype(BF16)

    def band_scores(h):
        ps = []
        for g in range(STEP_GROUPS):
            r0 = g * GROUP_ROWS
            s = lax.dot_general(qn_scr[h, r0:r0 + GROUP_ROWS, :], kw_scr[h, r0:r0 + WINDOW, :],
                                (((1,), (1,)), ((), ())), preferred_element_type=F32)
            s = s + bias_ref[h]
            ps.append(jnp.exp2(s - jnp.max(s, axis=-1, keepdims=True)).astype(BF16))
        return ps

    def band_values(h, ps):
        hs = slice(h * HEAD_DIM, (h + 1) * HEAD_DIM)
        for g, p in enumerate(ps):
            r0 = g * GROUP_ROWS
            rs = slice(r0, r0 + GROUP_ROWS)
            od = jnp.dot(p, vw_scr[h, r0:r0 + WINDOW, :], preferred_element_type=F32)
            o = od[:, 0:HEAD_DIM] / od[:, HEAD_DIM:]
            y_scr[rs, W_A + h * HEAD_DIM:W_A + (h + 1) * HEAD_DIM] = (
                o * _silu(bz_ref[rs, hs].astype(F32))).astype(BF16)

    def mem_norm(h):
        hs = slice(h * HEAD_DIM, (h + 1) * HEAD_DIM)
        mqn_scr[h] = _rms(mq_ref[:, hs].astype(F32), mqg_ref[...] * scale).astype(BF16)

    def mem_scores(h):
        mk = mk_ref[0, 0, :, h * HEAD_DIM:(h + 1) * HEAD_DIM]
        ps = []
        for r0 in range(0, t, MEM_ROWS):
            s = lax.dot_general(mqn_scr[h, r0:r0 + MEM_ROWS, :], mk, (((1,), (1,)), ((), ())),
                                preferred_element_type=F32)
            ps.append(jnp.exp2(s - jnp.max(s, axis=-1, keepdims=True)).astype(BF16))
        return ps

    def mem_values(h, ps):
        hs = slice(h * HEAD_DIM, (h + 1) * HEAD_DIM)
        mv = mv_ref[0, 0, h]
        for i, p in enumerate(ps):
            rs = slice(i * MEM_ROWS, (i + 1) * MEM_ROWS)
            od = jnp.dot(p, mv, preferred_element_type=F32)
            o = od[:, 0:HEAD_DIM] / od[:, HEAD_DIM:]
            c = W_A + W_B + h * HEAD_DIM
            y_scr[rs, c:c + HEAD_DIM] = (o * _silu(mz_ref[rs, hs].astype(F32))).astype(BF16)

    def merge(j):
        c = j * MERGE_COLS
        cs = slice(c, c + MERGE_COLS)
        gs = slice(c % GATE_BLOCK, c % GATE_BLOCK + MERGE_COLS)
        blk = c // GATE_BLOCK
        pa = jnp.dot(yp_scr[:, 0:W_A], wa_ref[:, cs], preferred_element_type=F32)
        acc = _sigmoid(gates[0][blk][:, gs].astype(F32) + bg_ref[0:1, cs]) * pa
        pb = jnp.dot(yp_scr[:, W_A:W_A + W_B], wb_ref[:, cs], preferred_element_type=F32)
        acc = acc + _sigmoid(gates[1][blk][:, gs].astype(F32) + bg_ref[1:2, cs]) * pb
        pm = jnp.dot(yp_scr[:, W_A + W_B:], wm_ref[:, cs], preferred_element_type=F32)
        acc = acc + _sigmoid(gates[2][blk][:, gs].astype(F32) + bg_ref[2:3, cs]) * pm
        m_scr[:, cs] = acc.astype(BF16)

    def out(j):
        cs = slice(j * OUT_COLS, (j + 1) * OUT_COLS)
        o_ref[:, cs] = x_ref[:, cs] + jnp.dot(mp_scr[...], wo_ref[:, cs], preferred_element_type=F32)

    heads = ([(band_norm, band_scores, band_values, h) for h in range(n_heads_b)]
             + [(mem_norm, mem_scores, mem_values, h) for h in range(n_heads_m)])
    fillers = ([functools.partial(merge, j) for j in range(d // MERGE_COLS)]
               + [functools.partial(out, j) for j in range(d // OUT_COLS)])
    assert len(fillers) == len(heads)
    conv_units = [conv_taps] + [functools.partial(conv_rows, r0) for r0 in range(0, t, CONV_ROWS)]
    pending = list(fillers)
    pending.pop(0)()
    for norm, _, _, h in heads[:NORM_AHEAD]:
        norm(h)
    ps_next = heads[0][1](heads[0][3])
    for i, (_, _, values, h) in enumerate(heads):
        ps = ps_next
        if i + NORM_AHEAD < len(heads):
            norm_ahead, _, _, h_ahead = heads[i + NORM_AHEAD]
            norm_ahead(h_ahead)
        if pending:
            pending.pop(0)()
        if i >= CONV_FIRST_SLOT and conv_units:
            conv_units.pop(0)()
        if i + 1 < len(heads):
            _, scores_next, _, h_next = heads[i + 1]
            ps_next = scores_next(h_next)
        values(h, ps)
    assert not conv_units and not pending


def _mix_merge_call(proj, xf, mk, mv, bias, conv_w, q_g, k_g, mq_g, b_gate, w_a, w_b, w_m, w_o,
                    layer, *, seq, cols):
    n_tok, d = xf.shape
    t = STEP
    n_tiles = n_tok // t
    tiles_per_seq = seq // t
    n_heads = W_B // HEAD_DIM
    n_heads_m = W_M // HEAD_DIM
    mem_len = mk.shape[2]

    def mix_tile(s):
        return jnp.minimum(s, n_tiles - 1)

    def merge_tile(s):
        return jnp.clip(s - 1, 0, n_tiles - 1)

    def out_tile(s):
        return jnp.clip(s - 2, 0, n_tiles - 1)

    def mix_spec(width, col):
        assert col % width == 0
        return pl.BlockSpec((t, width), lambda s: (mix_tile(s), col // width))

    assert cols["gate"] == 0
    resident = dict(pipeline_mode=pl.Buffered(1))
    in_specs = [
        mix_spec(4 * W_A, cols["conv"]), mix_spec(4 * W_B, cols["attn"]), mix_spec(2 * W_M, cols["mem"]),
        pl.BlockSpec((1, 1, mem_len, W_M), lambda s: (layer, mix_tile(s) // tiles_per_seq, 0, 0)),
        pl.BlockSpec((1, 1, n_heads_m, mem_len, 2 * HEAD_DIM),
                     lambda s: (layer, mix_tile(s) // tiles_per_seq, 0, 0, 0)),
        pl.BlockSpec((None, n_heads, GROUP_ROWS, WINDOW), lambda s: (layer, 0, 0, 0), **resident),
        pl.BlockSpec((None, CONV_WIDTH, W_A), lambda s: (layer, 0, 0)),
        pl.BlockSpec((None, 1, HEAD_DIM), lambda s: (layer, 0, 0)),
        pl.BlockSpec((None, 1, HEAD_DIM), lambda s: (layer, 0, 0)),
        pl.BlockSpec((None, 1, HEAD_DIM), lambda s: (layer, 0, 0)),
        pl.BlockSpec((t, 3 * d), lambda s: (merge_tile(s), 0)),
        pl.BlockSpec((t, d), lambda s: (out_tile(s), 0)),
        pl.BlockSpec((None, 3, d), lambda s: (layer, 0, 0)),
        pl.BlockSpec((None, W_A, d), lambda s: (layer, 0, 0), **resident),
        pl.BlockSpec((None, W_B, d), lambda s: (layer, 0, 0), **resident),
        pl.BlockSpec((None, W_M, d), lambda s: (layer, 0, 0), **resident),
        pl.BlockSpec((None, d, d), lambda s: (layer, 0, 0), **resident),
    ]
    return pl.pallas_call(
        functools.partial(_mix_merge_kernel, n_tiles=n_tiles, tiles_per_seq=tiles_per_seq),
        grid=(n_tiles + 2,),
        in_specs=in_specs,
        out_specs=pl.BlockSpec((t, d), lambda s: (out_tile(s), 0)),
        out_shape=jax.ShapeDtypeStruct((n_tok, d), F32),
        scratch_shapes=[
            pltpu.VMEM((t + CARRY_ROWS, W_A), F32),
            pltpu.VMEM((n_heads, HALO + t, KEY_LANES), BF16),
            pltpu.VMEM((n_heads, HALO + t, 2 * HEAD_DIM), BF16),
            pltpu.VMEM((n_heads, t, KEY_LANES), BF16),
            pltpu.VMEM((n_heads_m, t, HEAD_DIM), BF16),
            pltpu.VMEM((t, W_A + W_B + W_M), BF16),
            pltpu.VMEM((t, W_A + W_B + W_M), BF16),
            pltpu.VMEM((t, d), BF16),
            pltpu.VMEM((t, d), BF16),
        ],
        compiler_params=_params(1),
        name="mix_merge",
    )(proj, proj, proj, mk, mv, bias, conv_w, q_g, k_g, mq_g, proj, xf, b_gate, w_a, w_b, w_m, w_o)


def kernel(x, mem, norm_g, w_in, b_gate, conv_w, q_norm_g, k_norm_g, rel_table, mem_norm_g,
           w_mem_kv, mq_norm_g, mk_norm_g, w_branch_a, w_branch_b, w_branch_m, w_out):
    bsz, seq, d = x.shape
    n_layers = w_in.shape[0]
    n_tok = bsz * seq
    assert seq % HALO == 0 and HALO % STEP == 0 and d == 2 * GATE_BLOCK

    n_mix = 4 * W_A + 4 * W_B + 2 * W_M
    n_gate = 3 * d
    assert w_in.shape[-1] == n_mix + n_gate and n_gate % IN_COL_TILE == 0
    cols = {"gate": 0, "conv": n_gate, "attn": n_gate + 4 * W_A, "mem": n_gate + 4 * W_A + 4 * W_B}
    w_in_b = w_in.astype(BF16)
    w_kv = w_mem_kv.astype(BF16)
    w_a, w_b, w_m, w_o = (w.astype(BF16) for w in (w_branch_a, w_branch_b, w_branch_m, w_out))
    norm_g3 = norm_g.reshape(n_layers, 1, d)
    q_g = q_norm_g.reshape(n_layers, 1, HEAD_DIM)
    k_g = k_norm_g.reshape(n_layers, 1, HEAD_DIM)
    mq_g = mq_norm_g.reshape(n_layers, 1, HEAD_DIM)

    tm_in = min(IN_ROW_TILE, n_tok)
    tn_in = IN_COL_TILE

    bias = _bias_call(rel_table)
    mk, mv = _memkv_call(mem, mem_norm_g, w_kv, mk_norm_g)

    xf = x.reshape(n_tok, d)
    for layer in range(n_layers):
        proj = _inproj_call(xf, norm_g3, w_in_b, layer, tm=tm_in, tn=tn_in, col_shift=n_gate // tn_in)
        xf = _mix_merge_call(proj, xf, mk, mv, bias, conv_w, q_g, k_g, mq_g, b_gate,
                             w_a, w_b, w_m, w_o, layer, seq=seq, cols=cols)
    return xf.reshape(bsz, seq, d)
```
